```python
import jax, jax.numpy as jnp
from jax import lax
import numpy as np

D_MODEL = 1024
BATCH = 2
SEQ = 16384
DEPTH = 4

CHUNK = 64
EPS = 1e-6
GMLP_BLOCK = 128
A_HEADS = 4
A_WIDTH = D_MODEL // 2
A_HEAD_DIM = A_WIDTH // A_HEADS
POOL_WINDOWS = (2, 4, 8, 16)
B_GROUPS = len(POOL_WINDOWS)
B_WIDTH = D_MODEL // 2
B_GROUP_DIM = B_WIDTH // B_GROUPS
AB_IN_WIDTH = 2 * A_WIDTH + B_WIDTH
AB_OUT_WIDTH = A_WIDTH + B_WIDTH
RWKV_HEAD_DIM = 64
RWKV_HEADS = D_MODEL // RWKV_HEAD_DIM
DECAY_RANK = max(32, round(1.8 * D_MODEL ** 0.5 / 32) * 32)
AAA_RANK = max(32, round(1.8 * D_MODEL ** 0.5 / 32) * 32)
MV_RANK = max(32, round(1.3 * D_MODEL ** 0.5 / 32) * 32)
GATE_RANK = max(32, round(0.6 * D_MODEL ** 0.8 / 32) * 32)
GN_EPS = 64e-5
FFN_HIDDEN = ((8 * D_MODEL // 3 + 255) // 256) * 256
N_EVEN = (DEPTH + 1) // 2
N_ODD = DEPTH // 2

kernel_name = "hybrid_gmlp_pool_rwkv7_stream_encoder"


def rms_norm(x, g):
    xf = x.astype(jnp.float32)
    y = xf * lax.rsqrt(jnp.mean(xf * xf, axis=-1, keepdims=True) + EPS)
    return (y * g.astype(jnp.float32)).astype(x.dtype)


def gmlp_spatial_gate(u, v, sgu_gain, w_s, b_s):
    bsz, seq, _ = u.shape
    v = rms_norm(v, sgu_gain)
    v = v.reshape(bsz, seq // GMLP_BLOCK, GMLP_BLOCK, A_HEADS, A_HEAD_DIM)
    chunk_id = jnp.arange(GMLP_BLOCK) // CHUNK
    mask = chunk_id[:, None] >= chunk_id[None, :]
    w = jnp.where(mask[None], w_s, 0)
    mixed = jnp.einsum('hij,bnjhd->bnihd', w, v) + b_s.T[None, None, :, :, None]
    return u * mixed.reshape(bsz, seq, A_WIDTH)


def multiscale_pool(z, w_pool, pool_scale):
    bsz, seq, _ = z.shape
    zf = z.astype(jnp.float32).reshape(bsz, seq, B_GROUPS, B_GROUP_DIM)
    csum = jnp.cumsum(zf, axis=1)
    t = jnp.arange(seq)
    means = []
    for gi, win in enumerate(POOL_WINDOWS):
        cg = csum[:, :, gi]
        prev = jnp.pad(cg, ((0, 0), (win, 0), (0, 0)))[:, :seq]
        count = jnp.minimum(t + 1, win).astype(jnp.float32)[None, :, None]
        means.append((cg - prev) / count)
    pooled = (jnp.stack(means, axis=2) - zf).astype(z.dtype)
    y = jnp.einsum('bsgc,gcd->bsgd', pooled, w_pool).reshape(bsz, seq, B_WIDTH)
    return y * pool_scale


def hybrid_ab_mixer(hn, w_in, w_out, sgu_gain, sgu_w_s, sgu_bias, pool_w, pool_scale):
    z = hn @ w_in
    za = jax.nn.gelu(z[..., :2 * A_WIDTH], approximate=False)
    y_a = gmlp_spatial_gate(za[..., :A_WIDTH], za[..., A_WIDTH:], sgu_gain, sgu_w_s, sgu_bias)
    y_b = multiscale_pool(z[..., 2 * A_WIDTH:], pool_w, pool_scale)
    return jnp.concatenate([y_a, y_b], axis=-1) @ w_out


def wkv7_scan(r, w, k, v, a, b):
    bsz, _, nh, n = r.shape
    xs = tuple(jnp.moveaxis(t, 1, 0) for t in (r, w, k, v, a, b))

    def step(state, inp):
        r_t, w_t, k_t, v_t, a_t, b_t = inp
        sa = jnp.einsum('bhvk,bhk->bhv', state, a_t)
        state = (state * w_t[:, :, None, :] + sa[..., None] * b_t[:, :, None, :]
                 + v_t[..., None] * k_t[:, :, None, :])
        return state, jnp.einsum('bhvk,bhk->bhv', state, r_t)

    s0 = jnp.zeros((bsz, nh, n, n), jnp.float32)
    _, ys = lax.scan(step, s0, xs)
    return jnp.moveaxis(ys, 0, 1)


def rwkv7_time_mix(h, mu, w_r, w_k, w_v, w_o, w0, w1, w2, a0, a1, a2, g1, g2,
                   k_k, k_a, r_k, ln_w, ln_b, v_first, vres):
    bsz, seq, d = h.shape
    f32 = jnp.float32
    xx = jnp.pad(h, ((0, 0), (1, 0), (0, 0)))[:, :-1] - h
    xr, xw, xk, xv, xa, xg = [h + xx * mu[i] for i in range(6)]
    r = xr @ w_r
    wlog = -jax.nn.softplus(-(w0 + jnp.tanh(xw @ w1) @ w2).astype(f32)) - 0.5
    k = xk @ w_k
    v = xv @ w_v
    if vres is None:
        v_first = v
    else:
        v0, v1, v2 = vres
        v = v + (v_first - v) * jax.nn.sigmoid(v0 + (xv @ v1) @ v2)
    a = jax.nn.sigmoid(a0 + (xa @ a1) @ a2)
    g = jax.nn.sigmoid(xg @ g1) @ g2

    def heads(t):
        return t.reshape(bsz, seq, RWKV_HEADS, RWKV_HEAD_DIM).astype(f32)

    kk = heads(k * k_k)
    kk = kk / jnp.maximum(jnp.sqrt(jnp.sum(kk * kk, axis=-1, keepdims=True)), 1e-12)
    k = k * (1 + (a - 1) * k_a)
    rh, kh, vh, ah = heads(r), heads(k), heads(v), heads(a)
    decay = heads(jnp.exp(-jnp.exp(wlog)))
    y = wkv7_scan(rh, decay, kh, vh, -kk, kk * ah)
    mean = jnp.mean(y, axis=-1, keepdims=True)
    var = jnp.mean(jnp.square(y - mean), axis=-1, keepdims=True)
    y = ((y - mean) * lax.rsqrt(var + GN_EPS)).reshape(bsz, seq, d)
    y = y * ln_w.astype(f32) + ln_b.astype(f32)
    bonus = jnp.sum(rh * kh * r_k.astype(f32), axis=-1, keepdims=True) * vh
    y = (y + bonus.reshape(bsz, seq, d)).astype(h.dtype)
    return (y * g) @ w_o, v_first


def swiglu(hn, w_gate, w_up, w_down):
    return (jax.nn.silu(hn @ w_gate) * (hn @ w_up)) @ w_down


def setup_inputs(seed: int = 0) -> dict:
    key = jax.random.key(seed)
    ks = iter(jax.random.split(key, 48))
    D = D_MODEL

    def nrm(shape, scale):
        return scale * jax.random.normal(next(ks), shape, jnp.float32)

    def uni(shape, lo, hi):
        return jax.random.uniform(next(ks), shape, jnp.float32, lo, hi)

    nv = max(N_ODD - 1, 0)
    return {
        "x": nrm((BATCH, SEQ, D), 1.0),
        "mix_norm": 1.0 + nrm((DEPTH, D), 0.02),
        "ffn_norm": 1.0 + nrm((DEPTH, D), 0.02),
        "final_norm": 1.0 + nrm((D,), 0.02),
        "ab_w_in": nrm((N_EVEN, D, AB_IN_WIDTH), D ** -0.5),
        "ab_w_out": nrm((N_EVEN, AB_OUT_WIDTH, D), AB_OUT_WIDTH ** -0.5),
        "sgu_gain": 1.0 + nrm((N_EVEN, A_WIDTH), 0.02),
        "sgu_w_s": nrm((N_EVEN, A_HEADS, GMLP_BLOCK, GMLP_BLOCK), GMLP_BLOCK ** -0.5),
        "sgu_bias": 1.0 + nrm((N_EVEN, A_HEADS, GMLP_BLOCK), 0.1),
        "pool_w": nrm((N_EVEN, B_GROUPS, B_GROUP_DIM, B_GROUP_DIM), B_GROUP_DIM ** -0.5),
        "pool_scale": 0.5 + nrm((N_EVEN, B_WIDTH), 0.1),
        "rwkv_mu": uni((N_ODD, 6, D), 0.0, 1.0),
        "rwkv_w_r": nrm((N_ODD, D, D), D ** -0.5),
        "rwkv_w_k": nrm((N_ODD, D, D), D ** -0.5),
        "rwkv_w_v": nrm((N_ODD, D, D), D ** -0.5),
        "rwkv_w_o": nrm((N_ODD, D, D), D ** -0.5),
        "rwkv_w0": uni((N_ODD, D), -6.0, 0.0),
        "rwkv_w1": nrm((N_ODD, D, DECAY_RANK), D ** -0.5),
        "rwkv_w2": nrm((N_ODD, DECAY_RANK, D), 0.1 * DECAY_RANK ** -0.5),
        "rwkv_a0": nrm((N_ODD, D), 0.1),
        "rwkv_a1": nrm((N_ODD, D, AAA_RANK), D ** -0.5),
        "rwkv_a2": nrm((N_ODD, AAA_RANK, D), 0.1 * AAA_RANK ** -0.5),
        "rwkv_g1": nrm((N_ODD, D, GATE_RANK), D ** -0.5),
        "rwkv_g2": nrm((N_ODD, GATE_RANK, D), GATE_RANK ** -0.5),
        "rwkv_k_k": 0.85 + nrm((N_ODD, D), 0.05),
        "rwkv_k_a": 1.0 + nrm((N_ODD, D), 0.05),
        "rwkv_r_k": nrm((N_ODD, RWKV_HEADS, RWKV_HEAD_DIM), 0.1),
        "rwkv_ln_w": 1.0 + nrm((N_ODD, D), 0.05),
        "rwkv_ln_b": nrm((N_ODD, D), 0.02),
        "rwkv_v0": nrm((nv, D), 0.5),
        "rwkv_v1": nrm((nv, D, MV_RANK), D ** -0.5),
        "rwkv_v2": nrm((nv, MV_RANK, D), 0.1 * MV_RANK ** -0.5),
        "ffn_w_gate": nrm((DEPTH, D, FFN_HIDDEN), D ** -0.5),
        "ffn_w_up": nrm((DEPTH, D, FFN_HIDDEN), D ** -0.5),
        "ffn_w_down": nrm((DEPTH, FFN_HIDDEN, D), FFN_HIDDEN ** -0.5),
    }


def reference(x, mix_norm, ffn_norm, final_norm, ab_w_in, ab_w_out, sgu_gain, sgu_w_s,
              sgu_bias, pool_w, pool_scale, rwkv_mu, rwkv_w_r, rwkv_w_k, rwkv_w_v,
              rwkv_w_o, rwkv_w0, rwkv_w1, rwkv_w2, rwkv_a0, rwkv_a1, rwkv_a2, rwkv_g1,
              rwkv_g2, rwkv_k_k, rwkv_k_a, rwkv_r_k, rwkv_ln_w, rwkv_ln_b, rwkv_v0,
              rwkv_v1, rwkv_v2, ffn_w_gate, ffn_w_up, ffn_w_down):
    h = x
    v_first = None
    for layer in range(DEPTH):
        hn = rms_norm(h, mix_norm[layer])
        i = layer // 2
        if layer % 2 == 0:
            h = h + hybrid_ab_mixer(hn, ab_w_in[i], ab_w_out[i], sgu_gain[i], sgu_w_s[i],
                                    sgu_bias[i], pool_w[i], pool_scale[i])
        else:
            vres = None if i == 0 else (rwkv_v0[i - 1], rwkv_v1[i - 1], rwkv_v2[i - 1])
            y, v_first = rwkv7_time_mix(
                hn, rwkv_mu[i], rwkv_w_r[i], rwkv_w_k[i], rwkv_w_v[i], rwkv_w_o[i],
                rwkv_w0[i], rwkv_w1[i], rwkv_w2[i], rwkv_a0[i], rwkv_a1[i], rwkv_a2[i],
                rwkv_g1[i], rwkv_g2[i], rwkv_k_k[i], rwkv_k_a[i], rwkv_r_k[i],
                rwkv_ln_w[i], rwkv_ln_b[i], v_first, vres)
            h = h + y
        h = h + swiglu(rms_norm(h, ffn_norm[layer]), ffn_w_gate[layer], ffn_w_up[layer],
                       ffn_w_down[layer])
    return rms_norm(h, final_norm)
```

```python
import functools
import math

import jax
import jax.numpy as jnp
from jax import lax
from jax.experimental import pallas as pl
from jax.experimental.pallas import tpu as pltpu

F32 = jnp.float32
BF16 = jnp.bfloat16

EPS = 1e-6
GN_EPS = 64e-5
LANES = 128
HEAD_DIM = 64
WKV_CHUNK = 64
GMLP_BLOCK = 128
GMLP_CAUSAL_CHUNK = 64
POOL_WINDOWS = (2, 4, 8, 16)
POOL_HALO = 16
VMEM_LIMIT = 56 * 1024 * 1024


def _const_spec(shape):
    nd = len(shape)
    return pl.BlockSpec(shape, lambda *_: (0,) * nd, pipeline_mode=pl.Buffered(1))


def _rms(x, g):
    return x * lax.rsqrt(jnp.mean(x * x, axis=-1, keepdims=True) + EPS) * g


def _bdot(a, b):
    return jnp.dot(a.astype(BF16), b.astype(BF16), preferred_element_type=F32)


def _head_sum(x):
    lane_lo = lax.broadcasted_iota(jnp.int32, (1, LANES), 1) < HEAD_DIM
    outs = []
    for j in range(x.shape[1] // LANES):
        blk = x[:, j * LANES:(j + 1) * LANES]
        lo = jnp.sum(jnp.where(lane_lo, blk, 0.0), axis=-1, keepdims=True)
        tot = jnp.sum(blk, axis=-1, keepdims=True)
        outs.append(jnp.where(lane_lo, lo, tot - lo))
    return jnp.concatenate(outs, axis=1)


def _ab_kernel(x_ref, gmix_ref, win_ref, wout_ref, sgain_ref, ws_ref, sbias_ref, pw_ref,
               pscale_ref, o_ref, zs_ref, *, ts, a_width):
    s = pl.program_id(1)
    h = x_ref[...]
    hn = _rms(h, gmix_ref[...])
    z = _bdot(hn, win_ref[...])
    zg = z[:, :2 * a_width]
    za = 0.5 * zg * (1.0 + lax.erf(zg * (1.0 / math.sqrt(2.0))))
    u = za[:, :a_width]
    vn = _rms(za[:, a_width:], sgain_ref[...]).astype(BF16)

    ri = lax.broadcasted_iota(jnp.int32, (GMLP_BLOCK, GMLP_BLOCK), 0) // GMLP_CAUSAL_CHUNK
    ci = lax.broadcasted_iota(jnp.int32, (GMLP_BLOCK, GMLP_BLOCK), 1) // GMLP_CAUSAL_CHUNK
    keep = ri >= ci
    heads = []
    for hd in range(a_width // LANES):
        cs = slice(hd * LANES, (hd + 1) * LANES)
        w = jnp.where(keep, ws_ref[hd], 0.0).astype(BF16)
        bias = sbias_ref[:, cs]
        rows = []
        for blk in range(ts // GMLP_BLOCK):
            vb = vn[blk * GMLP_BLOCK:(blk + 1) * GMLP_BLOCK, cs]
            rows.append(jnp.dot(w, vb, preferred_element_type=F32) + bias)
        heads.append(jnp.concatenate(rows, axis=0))
    y_a = u * jnp.concatenate(heads, axis=1)

    zb = z[:, 2 * a_width:]

    @pl.when(s == 0)
    def _():
        zs_ref[0:POOL_HALO, :] = jnp.zeros((POOL_HALO, zb.shape[1]), F32)

    zs_ref[POOL_HALO:POOL_HALO + ts, :] = zb
    t1 = (lax.broadcasted_iota(jnp.int32, (ts, 1), 0) + s * ts + 1).astype(F32)
    groups = []
    for gi, win in enumerate(POOL_WINDOWS):
        cs = slice(gi * LANES, (gi + 1) * LANES)
        cur = zb[:, cs]
        acc = cur
        for j in range(1, win):
            acc = acc + zs_ref[pl.ds(POOL_HALO - j, ts), cs]
        pooled = acc / jnp.minimum(t1, float(win)) - cur
        groups.append(_bdot(pooled, pw_ref[gi]))
    y_b = jnp.concatenate(groups, axis=1) * pscale_ref[...]
    zs_ref[0:POOL_HALO, :] = zs_ref[ts:ts + POOL_HALO, :]

    cat = jnp.concatenate([y_a, y_b], axis=1)
    o_ref[...] = h + _bdot(cat, wout_ref[...])


def _ab_mixer(h, gmix, w_in, w_out, sgain, w_s, s_bias, pool_w, pool_scale, *, ts):
    bsz, seq, d = h.shape
    a_width = sgain.shape[0]
    b_width = pool_scale.shape[0]
    n_heads = w_s.shape[0]
    bias_full = jnp.repeat(s_bias.T, a_width // n_heads, axis=1)
    row = lambda v: v.reshape(1, -1)
    tok = pl.BlockSpec((None, ts, d), lambda b, s: (b, s, 0))
    return pl.pallas_call(
        functools.partial(_ab_kernel, ts=ts, a_width=a_width),
        grid=(bsz, seq // ts),
        in_specs=[tok, _const_spec((1, d)), _const_spec(w_in.shape), _const_spec(w_out.shape),
                  _const_spec((1, a_width)), _const_spec(w_s.shape), _const_spec(bias_full.shape),
                  _const_spec(pool_w.shape), _const_spec((1, b_width))],
        out_specs=tok,
        out_shape=jax.ShapeDtypeStruct(h.shape, F32),
        scratch_shapes=[pltpu.VMEM((ts + POOL_HALO, b_width), F32)],
        compiler_params=pltpu.CompilerParams(
            dimension_semantics=("arbitrary", "arbitrary"), vmem_limit_bytes=VMEM_LIMIT),
        name="ab_mixer",
    )(h, row(gmix), w_in.astype(BF16), w_out.astype(BF16), row(sgain), w_s, bias_full,
      pool_w.astype(BF16), row(pool_scale))


def _ffn_kernel(*refs, rwkv_post, final):
    it = iter(refs)
    x_ref = next(it)
    h = x_ref[...]
    if rwkv_post:
        y_ref, g_ref, bonus_ref, lnw_ref, lnb_ref, wo_ref = (next(it) for _ in range(6))
        y = y_ref[...]
        mean = _head_sum(y) * (1.0 / HEAD_DIM)
        dev = y - mean
        var = _head_sum(dev * dev) * (1.0 / HEAD_DIM)
        yn = dev * lax.rsqrt(var + GN_EPS) * lnw_ref[...] + lnb_ref[...]
        h = h + _bdot((yn + bonus_ref[...]) * g_ref[...], wo_ref[...])
    gn_ref, wg_ref, wu_ref, wd_ref = (next(it) for _ in range(4))
    fin_ref = next(it) if final else None
    o_ref = next(it)
    hn = _rms(h, gn_ref[...]).astype(BF16)
    gate = jnp.dot(hn, wg_ref[...], preferred_element_type=F32)
    up = jnp.dot(hn, wu_ref[...], preferred_element_type=F32)
    mid = gate * jax.nn.sigmoid(gate) * up
    out = h + _bdot(mid, wd_ref[...])
    if final:
        out = _rms(out, fin_ref[...])
    o_ref[...] = out


def _ffn(h, gnorm, w_gate, w_up, w_down, *, tm, post=None, final_norm=None):
    bsz, seq, d = h.shape
    n = bsz * seq
    tok = pl.BlockSpec((tm, d), lambda i: (i, 0))
    row = lambda v: v.reshape(1, -1)
    args = [h.reshape(n, d)]
    specs = [tok]
    if post is not None:
        y, g, bonus, ln_w, ln_b, w_o = post
        args += [y.reshape(n, d), g.reshape(n, d), bonus.reshape(n, d), row(ln_w), row(ln_b),
                 w_o.astype(BF16)]
        specs += [tok, tok, tok, _const_spec((1, d)), _const_spec((1, d)), _const_spec(w_o.shape)]
    args += [row(gnorm), w_gate.astype(BF16), w_up.astype(BF16), w_down.astype(BF16)]
    specs += [_const_spec((1, d)), _const_spec(w_gate.shape), _const_spec(w_up.shape),
              _const_spec(w_down.shape)]
    if final_norm is not None:
        args.append(row(final_norm))
        specs.append(_const_spec((1, d)))
    out = pl.pallas_call(
        functools.partial(_ffn_kernel, rwkv_post=post is not None, final=final_norm is not None),
        grid=(n // tm,),
        in_specs=specs,
        out_specs=tok,
        out_shape=jax.ShapeDtypeStruct((n, d), F32),
        compiler_params=pltpu.CompilerParams(
            dimension_semantics=("arbitrary",), vmem_limit_bytes=VMEM_LIMIT),
        name="ffn_post" if post is not None else "ffn",
    )(*args)
    return out.reshape(bsz, seq, d)


def _chunk_cumsum(x):
    rowmod = lax.broadcasted_iota(jnp.int32, (x.shape[0], 1), 0) % WKV_CHUNK
    d = 1
    while d < WKV_CHUNK:
        x = x + jnp.where(rowmod >= d, pltpu.roll(x, d, axis=0), 0.0)
        d *= 2
    return x


def _rwkv_pre_kernel(*refs, ts, vres):
    it = iter(refs)
    (x_ref, gmix_ref, mu_ref, wr_ref, wk_ref, wv_ref, w0_ref, w1_ref, w2_ref, a0_ref, a1_ref,
     a2_ref, g1_ref, g2_ref, kk_ref, ka_ref, rk_ref) = (next(it) for _ in range(17))
    if vres:
        vf_ref, v0_ref, v1_ref, v2_ref = (next(it) for _ in range(4))
    (rt_ref, at_ref, bt_ref, kt_ref, bl_ref, kl_ref, vb_ref, gl_ref, g_ref,
     bonus_ref) = (next(it) for _ in range(10))
    if not vres:
        vf_out_ref = next(it)
    prev_ref = next(it)

    s = pl.program_id(1)
    hn = _rms(x_ref[...], gmix_ref[...])

    @pl.when(s == 0)
    def _():
        prev_ref[...] = jnp.zeros(prev_ref.shape, F32)

    first = lax.broadcasted_iota(jnp.int32, (ts, 1), 0) == 0
    hprev = jnp.where(first, prev_ref[0:1, :], pltpu.roll(hn, 1, axis=0))
    prev_ref[0:1, :] = hn[ts - 1:ts, :]
    xx = hprev - hn
    mix = lambda i: (hn + xx * mu_ref[i:i + 1, :]).astype(BF16)
    xr, xw, xk, xv, xa, xg = (mix(i) for i in range(6))
    dot = lambda a, b_ref: jnp.dot(a, b_ref[...], preferred_element_type=F32)

    r = dot(xr, wr_ref)
    k = dot(xk, wk_ref)
    v = dot(xv, wv_ref)
    lw = -math.exp(-0.5) * jax.nn.sigmoid(
        w0_ref[...] + dot(jnp.tanh(dot(xw, w1_ref)).astype(BF16), w2_ref))
    a = jax.nn.sigmoid(a0_ref[...] + dot(dot(xa, a1_ref).astype(BF16), a2_ref))
    g_ref[...] = dot(jax.nn.sigmoid(dot(xg, g1_ref)).astype(BF16), g2_ref)
    if vres:
        v = v + (vf_ref[...] - v) * jax.nn.sigmoid(
            v0_ref[...] + dot(dot(xv, v1_ref).astype(BF16), v2_ref))
    else:
        vf_out_ref[...] = v

    kk = k * kk_ref[...]
    kk = kk / jnp.maximum(jnp.sqrt(_head_sum(kk * kk)), 1e-12)
    k = k * (1.0 + (a - 1.0) * ka_ref[...])
    bonus_ref[...] = _head_sum(r * k * rk_ref[...]) * v
    kka = kk * a

    cum = _chunk_cumsum(lw)
    nck = ts // WKV_CHUNK
    d = cum.shape[1]
    cum3 = cum.reshape(nck, WKV_CHUNK, d)
    last = cum3[:, WKV_CHUNK - 1:WKV_CHUNK, :]
    gl_ref[...] = jnp.exp(last)
    to_end = jnp.exp(last - cum3).reshape(ts, d)
    inv = jnp.exp(-cum)
    rt_ref[...] = (r * jnp.exp(cum)).astype(BF16)
    at_ref[...] = (-kk * jnp.exp(cum - lw)).astype(BF16)
    bt_ref[...] = (kka * inv).astype(BF16)
    kt_ref[...] = (k * inv).astype(BF16)
    bl_ref[...] = (kka * to_end).astype(BF16)
    kl_ref[...] = (k * to_end).astype(BF16)
    vb_ref[...] = v.astype(BF16)


def _rwkv_pre(h, gmix, p, v_first, vres, *, ts):
    bsz, seq, d = h.shape
    row = lambda v: v.reshape(1, -1)
    tok = pl.BlockSpec((None, ts, d), lambda b, s: (b, s, 0))
    nck = ts // WKV_CHUNK
    wb = lambda n: p[n].astype(BF16)
    args = [h, row(gmix), p["mu"], wb("w_r"), wb("w_k"), wb("w_v"), row(p["w0"]), wb("w1"),
            wb("w2"), row(p["a0"]), wb("a1"), wb("a2"), wb("g1"), wb("g2"), row(p["k_k"]),
            row(p["k_a"]), row(p["r_k"])]
    specs = [tok] + [_const_spec(a.shape) for a in args[1:]]
    if vres is not None:
        v0, v1, v2 = vres
        extra = [row(v0), v1.astype(BF16), v2.astype(BF16)]
        args += [v_first] + extra
        specs += [tok] + [_const_spec(a.shape) for a in extra]
    tok_bf = jax.ShapeDtypeStruct(h.shape, BF16)
    tok_f32 = jax.ShapeDtypeStruct(h.shape, F32)
    out_shape = [tok_bf] * 7 + [jax.ShapeDtypeStruct((bsz, seq // WKV_CHUNK, 1, d), F32),
                                tok_f32, tok_f32]
    out_specs = [tok] * 7 + [pl.BlockSpec((None, nck, 1, d), lambda b, s: (b, s, 0, 0)), tok, tok]
    if vres is None:
        out_shape.append(tok_f32)
        out_specs.append(tok)
    outs = pl.pallas_call(
        functools.partial(_rwkv_pre_kernel, ts=ts, vres=vres is not None),
        grid=(bsz, seq // ts),
        in_specs=specs,
        out_specs=out_specs,
        out_shape=out_shape,
        scratch_shapes=[pltpu.VMEM((8, d), F32)],
        compiler_params=pltpu.CompilerParams(
            dimension_semantics=("arbitrary", "arbitrary"), vmem_limit_bytes=VMEM_LIMIT),
        name="rwkv_pre_vres" if vres is not None else "rwkv_pre",
    )(*args)
    scan_in, gate, bonus = outs[:8], outs[8], outs[9]
    if vres is None:
        v_first = outs[10]
    return scan_in, gate, bonus, v_first


def _wkv_kernel(rt_ref, at_ref, bt_ref, kt_ref, bl_ref, kl_ref, v_ref, gl_ref, y_ref, s_ref,
                *, npair, nchunk):
    pair = 2 * WKV_CHUNK

    @pl.when(pl.program_id(2) == 0)
    def _():
        s_ref[...] = jnp.zeros(s_ref.shape, F32)

    row = lax.broadcasted_iota(jnp.int32, (pair, pair), 0)
    col = lax.broadcasted_iota(jnp.int32, (pair, pair), 1)
    same = (row // WKV_CHUNK) == (col // WKV_CHUNK)
    strict = same & ((row % WKV_CHUNK) > (col % WKV_CHUNK))
    incl = same & ((row % WKV_CHUNK) >= (col % WKV_CHUNK))
    eye = jnp.where(row == col, 1.0, 0.0).astype(F32)
    same_bf = jnp.where(same, 1.0, 0.0).astype(BF16)
    nt = (((1,), (1,)), ((), ()))
    tn = (((0,), (0,)), ((), ()))

    def stack(x):
        return jnp.concatenate([x, x], axis=0) * same_bf

    for p in range(npair):
        ls = slice(p * LANES, (p + 1) * LANES)
        state = s_ref[p]
        for ck in range(nchunk):
            rs = slice(ck * WKV_CHUNK, (ck + 1) * WKV_CHUNK)
            bt = bt_ref[rs, ls]
            kt = kt_ref[rs, ls]
            v_ms = stack(v_ref[rs, ls])
            lhs = jnp.concatenate([stack(at_ref[rs, ls]), stack(rt_ref[rs, ls])], axis=0)
            rhs = jnp.concatenate([bt, bt, kt, kt], axis=0)
            amat = lax.dot_general(lhs, rhs, nt, preferred_element_type=F32)
            a_ab = jnp.where(strict, amat[:pair, :pair], 0.0)
            a_ak = jnp.where(strict, amat[:pair, pair:], 0.0)
            a_rb = jnp.where(incl, amat[pair:, :pair], 0.0)
            a_rk = jnp.where(incl, amat[pair:, pair:], 0.0)

            power = a_ab
            tinv = eye + a_ab
            for _ in range(int(math.log2(WKV_CHUNK)) - 1):
                pb = power.astype(BF16)
                power = jnp.dot(pb, pb, preferred_element_type=F32)
                tinv = tinv + _bdot(tinv, power)

            ar = lax.dot_general(lhs, state.astype(BF16), nt, preferred_element_type=F32)
            x = ar[:pair] + _bdot(a_ak, v_ms)
            u = _bdot(tinv, x)
            uv = jnp.concatenate([u.astype(BF16), v_ms], axis=0)
            y_ms = ar[pair:] + _bdot(jnp.concatenate([a_rb, a_rk], axis=1), uv)
            y_ref[rs, ls] = y_ms[:WKV_CHUNK] + y_ms[WKV_CHUNK:]
            bk = jnp.concatenate([stack(bl_ref[rs, ls]), stack(kl_ref[rs, ls])], axis=0)
            state = state * gl_ref[ck, :, ls] + lax.dot_general(
                uv, bk, tn, preferred_element_type=F32)
        s_ref[p] = state


def _wkv(scan_in, *, npair, nchunk):
    rt = scan_in[0]
    bsz, seq, d = rt.shape
    width = npair * LANES
    rows = nchunk * WKV_CHUNK
    tok = pl.BlockSpec((None, rows, width), lambda b, p, c: (b, c, p))
    gls = pl.BlockSpec((None, nchunk, 1, width), lambda b, p, c: (b, c, 0, p))
    return pl.pallas_call(
        functools.partial(_wkv_kernel, npair=npair, nchunk=nchunk),
        grid=(bsz, d // width, seq // rows),
        in_specs=[tok] * 7 + [gls],
        out_specs=tok,
        out_shape=jax.ShapeDtypeStruct((bsz, seq, d), F32),
        scratch_shapes=[pltpu.VMEM((npair, LANES, LANES), F32)],
        compiler_params=pltpu.CompilerParams(
            dimension_semantics=("arbitrary", "arbitrary", "arbitrary"),
            vmem_limit_bytes=VMEM_LIMIT),
        name="wkv",
    )(*scan_in)


def kernel(x, mix_norm, ffn_norm, final_norm, ab_w_in, ab_w_out, sgu_gain, sgu_w_s, sgu_bias, pool_w, pool_scale, rwkv_mu, rwkv_w_r, rwkv_w_k, rwkv_w_v, rwkv_w_o, rwkv_w0, rwkv_w1, rwkv_w2, rwkv_a0, rwkv_a1, rwkv_a2, rwkv_g1, rwkv_g2, rwkv_k_k, rwkv_k_a, rwkv_r_k, rwkv_ln_w, rwkv_ln_b, rwkv_v0, rwkv_v1, rwkv_v2, ffn_w_gate, ffn_w_up, ffn_w_down):
    depth = mix_norm.shape[0]
    seq = x.shape[1]
    ts_ab = min(512, seq)
    ts_pre = min(256, seq)
    tm = min(512, seq)
    h = x
    v_first = None
    for layer in range(depth):
        i = layer // 2
        post = None
        if layer % 2 == 0:
            h = _ab_mixer(h, mix_norm[layer], ab_w_in[i], ab_w_out[i], sgu_gain[i], sgu_w_s[i],
                          sgu_bias[i], pool_w[i], pool_scale[i], ts=ts_ab)
        else:
            p = dict(mu=rwkv_mu[i], w_r=rwkv_w_r[i], w_k=rwkv_w_k[i], w_v=rwkv_w_v[i],
                     w0=rwkv_w0[i], w1=rwkv_w1[i], w2=rwkv_w2[i], a0=rwkv_a0[i], a1=rwkv_a1[i],
                     a2=rwkv_a2[i], g1=rwkv_g1[i], g2=rwkv_g2[i], k_k=rwkv_k_k[i],
                     k_a=rwkv_k_a[i], r_k=rwkv_r_k[i])
            vres = None if i == 0 else (rwkv_v0[i - 1], rwkv_v1[i - 1], rwkv_v2[i - 1])
            scan_in, gate, bonus, v_first = _rwkv_pre(h, mix_norm[layer], p, v_first, vres,
                                                       ts=ts_pre)
            y = _wkv(scan_in, npair=2, nchunk=4)
            post = (y, gate, bonus, rwkv_ln_w[i], rwkv_ln_b[i], rwkv_w_o[i])
        h = _ffn(h, ffn_norm[layer], ffn_w_gate[layer], ffn_w_up[layer], ffn_w_down[layer],
                 tm=tm, post=post, final_norm=final_norm if layer == depth - 1 else None)
    return h
```

```python
import functools
import math

import jax
import jax.numpy as jnp
from jax import lax
from jax.experimental import pallas as pl
from jax.experimental.pallas import tpu as pltpu

F32 = jnp.float32
BF16 = jnp.bfloat16

EPS = 1e-6
GN_EPS = 64e-5
LANES = 128
HEAD_DIM = 64
WKV_CHUNK = 64
GMLP_BLOCK = 128
GMLP_CAUSAL_CHUNK = 64
POOL_WINDOWS = (2, 4, 8, 16)
POOL_HALO = 16
VMEM_LIMIT = 56 * 1024 * 1024


def _const_spec(shape):
    nd = len(shape)
    return pl.BlockSpec(shape, lambda *_: (0,) * nd, pipeline_mode=pl.Buffered(1))


def _rms(x, g):
    return x * lax.rsqrt(jnp.mean(x * x, axis=-1, keepdims=True) + EPS) * g


def _bdot(a, b):
    return jnp.dot(a.astype(BF16), b.astype(BF16), preferred_element_type=F32)


def _head_sum(x):
    lane_lo = lax.broadcasted_iota(jnp.int32, (1, LANES), 1) < HEAD_DIM
    outs = []
    for j in range(x.shape[1] // LANES):
        blk = x[:, j * LANES:(j + 1) * LANES]
        lo = jnp.sum(jnp.where(lane_lo, blk, 0.0), axis=-1, keepdims=True)
        tot = jnp.sum(blk, axis=-1, keepdims=True)
        outs.append(jnp.where(lane_lo, lo, tot - lo))
    return jnp.concatenate(outs, axis=1)


def _ab_kernel(x_ref, gmix_ref, win_ref, wout_ref, sgain_ref, ws_ref, sbias_ref, pw_ref,
               pscale_ref, o_ref, zs_ref, *, ts, a_width):
    s = pl.program_id(1)
    h = x_ref[...]
    hn = _rms(h, gmix_ref[...])
    z = _bdot(hn, win_ref[...])
    zg = z[:, :2 * a_width]
    za = 0.5 * zg * (1.0 + lax.erf(zg * (1.0 / math.sqrt(2.0))))
    u = za[:, :a_width]
    vn = _rms(za[:, a_width:], sgain_ref[...]).astype(BF16)

    ri = lax.broadcasted_iota(jnp.int32, (GMLP_BLOCK, GMLP_BLOCK), 0) // GMLP_CAUSAL_CHUNK
    ci = lax.broadcasted_iota(jnp.int32, (GMLP_BLOCK, GMLP_BLOCK), 1) // GMLP_CAUSAL_CHUNK
    keep = ri >= ci
    heads = []
    for hd in range(a_width // LANES):
        cs = slice(hd * LANES, (hd + 1) * LANES)
        w = jnp.where(keep, ws_ref[hd], 0.0).astype(BF16)
        bias = sbias_ref[:, cs]
        rows = []
        for blk in range(ts // GMLP_BLOCK):
            vb = vn[blk * GMLP_BLOCK:(blk + 1) * GMLP_BLOCK, cs]
            rows.append(jnp.dot(w, vb, preferred_element_type=F32) + bias)
        heads.append(jnp.concatenate(rows, axis=0))
    y_a = u * jnp.concatenate(heads, axis=1)

    zb = z[:, 2 * a_width:]

    @pl.when(s == 0)
    def _():
        zs_ref[0:POOL_HALO, :] = jnp.zeros((POOL_HALO, zb.shape[1]), F32)

    zs_ref[POOL_HALO:POOL_HALO + ts, :] = zb
    t1 = (lax.broadcasted_iota(jnp.int32, (ts, 1), 0) + s * ts + 1).astype(F32)
    groups = []
    for gi, win in enumerate(POOL_WINDOWS):
        cs = slice(gi * LANES, (gi + 1) * LANES)
        cur = zb[:, cs]
        acc = cur
        for j in range(1, win):
            acc = acc + zs_ref[pl.ds(POOL_HALO - j, ts), cs]
        pooled = acc / jnp.minimum(t1, float(win)) - cur
        groups.append(_bdot(pooled, pw_ref[gi]))
    y_b = jnp.concatenate(groups, axis=1) * pscale_ref[...]
    zs_ref[0:POOL_HALO, :] = zs_ref[ts:ts + POOL_HALO, :]

    cat = jnp.concatenate([y_a, y_b], axis=1)
    o_ref[...] = h + _bdot(cat, wout_ref[...])


def _ab_mixer(h, gmix, w_in, w_out, sgain, w_s, s_bias, pool_w, pool_scale, *, ts):
    bsz, seq, d = h.shape
    a_width = sgain.shape[0]
    b_width = pool_scale.shape[0]
    n_heads = w_s.shape[0]
    bias_full = jnp.repeat(s_bias.T, a_width // n_heads, axis=1)
    row = lambda v: v.reshape(1, -1)
    tok = pl.BlockSpec((None, ts, d), lambda b, s: (b, s, 0))
    return pl.pallas_call(
        functools.partial(_ab_kernel, ts=ts, a_width=a_width),
        grid=(bsz, seq // ts),
        in_specs=[tok, _const_spec((1, d)), _const_spec(w_in.shape), _const_spec(w_out.shape),
                  _const_spec((1, a_width)), _const_spec(w_s.shape), _const_spec(bias_full.shape),
                  _const_spec(pool_w.shape), _const_spec((1, b_width))],
        out_specs=tok,
        out_shape=jax.ShapeDtypeStruct(h.shape, F32),
        scratch_shapes=[pltpu.VMEM((ts + POOL_HALO, b_width), F32)],
        compiler_params=pltpu.CompilerParams(
            dimension_semantics=("arbitrary", "arbitrary"), vmem_limit_bytes=VMEM_LIMIT),
        name="ab_mixer",
    )(h, row(gmix), w_in.astype(BF16), w_out.astype(BF16), row(sgain), w_s, bias_full,
      pool_w.astype(BF16), row(pool_scale))


def _ffn_kernel(*refs, rwkv_post, final):
    it = iter(refs)
    x_ref = next(it)
    h = x_ref[...]
    if rwkv_post:
        y_ref, g_ref, bonus_ref, lnw_ref, lnb_ref, wo_ref = (next(it) for _ in range(6))
        y = y_ref[...]
        mean = _head_sum(y) * (1.0 / HEAD_DIM)
        dev = y - mean
        var = _head_sum(dev * dev) * (1.0 / HEAD_DIM)
        yn = dev * lax.rsqrt(var + GN_EPS) * lnw_ref[...] + lnb_ref[...]
        h = h + _bdot((yn + bonus_ref[...]) * g_ref[...], wo_ref[...])
    gn_ref, wg_ref, wu_ref, wd_ref = (next(it) for _ in range(4))
    fin_ref = next(it) if final else None
    o_ref = next(it)
    hn = _rms(h, gn_ref[...]).astype(BF16)
    gate = jnp.dot(hn, wg_ref[...], preferred_element_type=F32)
    up = jnp.dot(hn, wu_ref[...], preferred_element_type=F32)
    mid = gate * jax.nn.sigmoid(gate) * up
    out = h + _bdot(mid, wd_ref[...])
    if final:
        out = _rms(out, fin_ref[...])
    o_ref[...] = out


def _ffn(h, gnorm, w_gate, w_up, w_down, *, tm, post=None, final_norm=None):
    bsz, seq, d = h.shape
    n = bsz * seq
    tok = pl.BlockSpec((tm, d), lambda i: (i, 0))
    row = lambda v: v.reshape(1, -1)
    args = [h.reshape(n, d)]
    specs = [tok]
    if post is not None:
        y, g, bonus, ln_w, ln_b, w_o = post
        args += [y.reshape(n, d), g.reshape(n, d), bonus.reshape(n, d), row(ln_w), row(ln_b),
                 w_o.astype(BF16)]
        specs += [tok, tok, tok, _const_spec((1, d)), _const_spec((1, d)), _const_spec(w_o.shape)]
    args += [row(gnorm), w_gate.astype(BF16), w_up.astype(BF16), w_down.astype(BF16)]
    specs += [_const_spec((1, d)), _const_spec(w_gate.shape), _const_spec(w_up.shape),
              _const_spec(w_down.shape)]
    if final_norm is not None:
        args.append(row(final_norm))
        specs.append(_const_spec((1, d)))
    out = pl.pallas_call(
        functools.partial(_ffn_kernel, rwkv_post=post is not None, final=final_norm is not None),
        grid=(n // tm,),
        in_specs=specs,
        out_specs=tok,
        out_shape=jax.ShapeDtypeStruct((n, d), F32),
        compiler_params=pltpu.CompilerParams(
            dimension_semantics=("arbitrary",), vmem_limit_bytes=VMEM_LIMIT),
        name="ffn_post" if post is not None else "ffn",
    )(*args)
    return out.reshape(bsz, seq, d)


def _chunk_cumsum(x):
    rowmod = lax.broadcasted_iota(jnp.int32, (x.shape[0], 1), 0) % WKV_CHUNK
    d = 1
    while d < WKV_CHUNK:
        x = x + jnp.where(rowmod >= d, pltpu.roll(x, d, axis=0), 0.0)
        d *= 2
    return x


def _rwkv_pre_kernel(*refs, ts, vres):
    it = iter(refs)
    (x_ref, gmix_ref, mu_ref, wr_ref, wk_ref, wv_ref, w0_ref, w1_ref, w2_ref, a0_ref, a1_ref,
     a2_ref, g1_ref, g2_ref, kk_ref, ka_ref, rk_ref) = (next(it) for _ in range(17))
    if vres:
        vf_ref, v0_ref, v1_ref, v2_ref = (next(it) for _ in range(4))
    (rt_ref, at_ref, bt_ref, kt_ref, bl_ref, kl_ref, vb_ref, gl_ref, g_ref,
     bonus_ref) = (next(it) for _ in range(10))
    if not vres:
        vf_out_ref = next(it)
    prev_ref = next(it)

    s = pl.program_id(1)
    hn = _rms(x_ref[...], gmix_ref[...])

    @pl.when(s == 0)
    def _():
        prev_ref[...] = jnp.zeros(prev_ref.shape, F32)

    first = lax.broadcasted_iota(jnp.int32, (ts, 1), 0) == 0
    hprev = jnp.where(first, prev_ref[0:1, :], pltpu.roll(hn, 1, axis=0))
    prev_ref[0:1, :] = hn[ts - 1:ts, :]
    xx = hprev - hn
    mix = lambda i: (hn + xx * mu_ref[i:i + 1, :]).astype(BF16)
    xr, xw, xk, xv, xa, xg = (mix(i) for i in range(6))
    dot = lambda a, b_ref: jnp.dot(a, b_ref[...], preferred_element_type=F32)

    r = dot(xr, wr_ref)
    k = dot(xk, wk_ref)
    v = dot(xv, wv_ref)
    lw = -math.exp(-0.5) * jax.nn.sigmoid(
        w0_ref[...] + dot(jnp.tanh(dot(xw, w1_ref)).astype(BF16), w2_ref))
    a = jax.nn.sigmoid(a0_ref[...] + dot(dot(xa, a1_ref).astype(BF16), a2_ref))
    g_ref[...] = dot(jax.nn.sigmoid(dot(xg, g1_ref)).astype(BF16), g2_ref)
    if vres:
        v = v + (vf_ref[...] - v) * jax.nn.sigmoid(
            v0_ref[...] + dot(dot(xv, v1_ref).astype(BF16), v2_ref))
    else:
        vf_out_ref[...] = v

    kk = k * kk_ref[...]
    kk = kk / jnp.maximum(jnp.sqrt(_head_sum(kk * kk)), 1e-12)
    k = k * (1.0 + (a - 1.0) * ka_ref[...])
    bonus_ref[...] = _head_sum(r * k * rk_ref[...]) * v
    kka = kk * a

    cum = _chunk_cumsum(lw)
    nck = ts // WKV_CHUNK
    d = cum.shape[1]
    cum3 = cum.reshape(nck, WKV_CHUNK, d)
    last = cum3[:, WKV_CHUNK - 1:WKV_CHUNK, :]
    gl_ref[...] = jnp.exp(last)
    to_end = jnp.exp(last - cum3).reshape(ts, d)
    inv = jnp.exp(-cum)
    rt_ref[...] = (r * jnp.exp(cum)).astype(BF16)
    at_ref[...] = (-kk * jnp.exp(cum - lw)).astype(BF16)
    bt_ref[...] = (kka * inv).astype(BF16)
    kt_ref[...] = (k * inv).astype(BF16)
    bl_ref[...] = (kka * to_end).astype(BF16)
    kl_ref[...] = (k * to_end).astype(BF16)
    vb_ref[...] = v.astype(BF16)


def _rwkv_pre(h, gmix, p, v_first, vres, *, ts):
    bsz, seq, d = h.shape
    row = lambda v: v.reshape(1, -1)
    tok = pl.BlockSpec((None, ts, d), lambda b, s: (b, s, 0))
    nck = ts // WKV_CHUNK
    wb = lambda n: p[n].astype(BF16)
    args = [h, row(gmix), p["mu"], wb("w_r"), wb("w_k"), wb("w_v"), row(p["w0"]), wb("w1"),
            wb("w2"), row(p["a0"]), wb("a1"), wb("a2"), wb("g1"), wb("g2"), row(p["k_k"]),
            row(p["k_a"]), row(p["r_k"])]
    specs = [tok] + [_const_spec(a.shape) for a in args[1:]]
    if vres is not None:
        v0, v1, v2 = vres
        extra = [row(v0), v1.astype(BF16), v2.astype(BF16)]
        args += [v_first] + extra
        specs += [tok] + [_const_spec(a.shape) for a in extra]
    tok_bf = jax.ShapeDtypeStruct(h.shape, BF16)
    tok_f32 = jax.ShapeDtypeStruct(h.shape, F32)
    out_shape = [tok_bf] * 7 + [jax.ShapeDtypeStruct((bsz, seq // WKV_CHUNK, 1, d), F32),
                                tok_f32, tok_f32]
    out_specs = [tok] * 7 + [pl.BlockSpec((None, nck, 1, d), lambda b, s: (b, s, 0, 0)), tok, tok]
    if vres is None:
        out_shape.append(tok_f32)
        out_specs.append(tok)
    outs = pl.pallas_call(
        functools.partial(_rwkv_pre_kernel, ts=ts, vres=vres is not None),
        grid=(bsz, seq // ts),
        in_specs=specs,
        out_specs=out_specs,
        out_shape=out_shape,
        scratch_shapes=[pltpu.VMEM((8, d), F32)],
        compiler_params=pltpu.CompilerParams(
            dimension_semantics=("arbitrary", "arbitrary"), vmem_limit_bytes=VMEM_LIMIT),
        name="rwkv_pre_vres" if vres is not None else "rwkv_pre",
    )(*args)
    scan_in, gate, bonus = outs[:8], outs[8], outs[9]
    if vres is None:
        v_first = outs[10]
    return scan_in, gate, bonus, v_first


def _wkv_kernel(rt_ref, at_ref, bt_ref, kt_ref, bl_ref, kl_ref, v_ref, gl_ref, y_ref, s_ref,
                *, npair, nchunk):
    pair = 2 * WKV_CHUNK

    @pl.when(pl.program_id(2) == 0)
    def _():
        s_ref[...] = jnp.zeros(s_ref.shape, F32)

    row = lax.broadcasted_iota(jnp.int32, (pair, pair), 0)
    col = lax.broadcasted_iota(jnp.int32, (pair, pair), 1)
    same = (row // WKV_CHUNK) == (col // WKV_CHUNK)
    strict = same & ((row % WKV_CHUNK) > (col % WKV_CHUNK))
    incl = same & ((row % WKV_CHUNK) >= (col % WKV_CHUNK))
    eye = jnp.where(row == col, 1.0, 0.0).astype(F32)
    same_bf = jnp.where(same, 1.0, 0.0).astype(BF16)
    nt = (((1,), (1,)), ((), ()))
    tn = (((0,), (0,)), ((), ()))

    def stack(x):
        return jnp.concatenate([x, x], axis=0) * same_bf

    items = [(ck, p) for ck in range(nchunk) for p in range(npair)]
    sl = {(ck, p): (slice(ck * WKV_CHUNK, (ck + 1) * WKV_CHUNK), slice(p * LANES, (p + 1) * LANES))
          for ck, p in items}
    lhs, v_ms, a_ak, a_rbk, power, tinv = {}, {}, {}, {}, {}, {}
    for it in items:
        rs, ls = sl[it]
        bt = bt_ref[rs, ls]
        kt = kt_ref[rs, ls]
        v_ms[it] = stack(v_ref[rs, ls])
        lhs[it] = jnp.concatenate([stack(at_ref[rs, ls]), stack(rt_ref[rs, ls])], axis=0)
        rhs = jnp.concatenate([bt, bt, kt, kt], axis=0)
        amat = lax.dot_general(lhs[it], rhs, nt, preferred_element_type=F32)
        a_ab = jnp.where(strict, amat[:pair, :pair], 0.0)
        a_ak[it] = jnp.where(strict, amat[:pair, pair:], 0.0).astype(BF16)
        a_rbk[it] = jnp.concatenate(
            [jnp.where(incl, amat[pair:, :pair], 0.0), jnp.where(incl, amat[pair:, pair:], 0.0)],
            axis=1).astype(BF16)
        power[it] = a_ab
        tinv[it] = eye + a_ab

    for _ in range(int(math.log2(WKV_CHUNK)) - 1):
        for it in items:
            pb = power[it].astype(BF16)
            power[it] = jnp.dot(pb, pb, preferred_element_type=F32)
            tinv[it] = tinv[it] + _bdot(tinv[it], power[it])

    xv = {it: jnp.dot(a_ak[it], v_ms[it], preferred_element_type=F32) for it in items}
    tinv = {it: tinv[it].astype(BF16) for it in items}

    state = [s_ref[p] for p in range(npair)]
    for ck in range(nchunk):
        ar, uv = {}, {}
        for p in range(npair):
            ar[p] = lax.dot_general(lhs[ck, p], state[p].astype(BF16), nt,
                                    preferred_element_type=F32)
        for p in range(npair):
            u = jnp.dot(tinv[ck, p], (ar[p][:pair] + xv[ck, p]).astype(BF16),
                        preferred_element_type=F32)
            uv[p] = jnp.concatenate([u.astype(BF16), v_ms[ck, p]], axis=0)
        for p in range(npair):
            rs, ls = sl[ck, p]
            y_ms = ar[p][pair:] + jnp.dot(a_rbk[ck, p], uv[p], preferred_element_type=F32)
            y_ref[rs, ls] = y_ms[:WKV_CHUNK] + y_ms[WKV_CHUNK:]
            bk = jnp.concatenate([stack(bl_ref[rs, ls]), stack(kl_ref[rs, ls])], axis=0)
            state[p] = state[p] * gl_ref[ck, :, ls] + lax.dot_general(
                uv[p], bk, tn, preferred_element_type=F32)
    for p in range(npair):
        s_ref[p] = state[p]


def _wkv(scan_in, *, npair, nchunk):
    rt = scan_in[0]
    bsz, seq, d = rt.shape
    width = npair * LANES
    rows = nchunk * WKV_CHUNK
    tok = pl.BlockSpec((None, rows, width), lambda b, p, c: (b, c, p))
    gls = pl.BlockSpec((None, nchunk, 1, width), lambda b, p, c: (b, c, 0, p))
    return pl.pallas_call(
        functools.partial(_wkv_kernel, npair=npair, nchunk=nchunk),
        grid=(bsz, d // width, seq // rows),
        in_specs=[tok] * 7 + [gls],
        out_specs=tok,
        out_shape=jax.ShapeDtypeStruct((bsz, seq, d), F32),
        scratch_shapes=[pltpu.VMEM((npair, LANES, LANES), F32)],
        compiler_params=pltpu.CompilerParams(
            dimension_semantics=("arbitrary", "arbitrary", "arbitrary"),
            vmem_limit_bytes=VMEM_LIMIT),
        name="wkv",
    )(*scan_in)


def kernel(x, mix_norm, ffn_norm, final_norm, ab_w_in, ab_w_out, sgu_gain, sgu_w_s, sgu_bias, pool_w, pool_scale, rwkv_mu, rwkv_w_r, rwkv_w_k, rwkv_w_v, rwkv_w_o, rwkv_w0, rwkv_w1, rwkv_w2, rwkv_a0, rwkv_a1, rwkv_a2, rwkv_g1, rwkv_g2, rwkv_k_k, rwkv_k_a, rwkv_r_k, rwkv_ln_w, rwkv_ln_b, rwkv_v0, rwkv_v1, rwkv_v2, ffn_w_gate, ffn_w_up, ffn_w_down):
    depth = mix_norm.shape[0]
    seq = x.shape[1]
    ts_ab = min(512, seq)
    ts_pre = min(256, seq)
    tm = min(512, seq)
    h = x
    v_first = None
    for layer in range(depth):
        i = layer // 2
        post = None
        if layer % 2 == 0:
            h = _ab_mixer(h, mix_norm[layer], ab_w_in[i], ab_w_out[i], sgu_gain[i], sgu_w_s[i],
                          sgu_bias[i], pool_w[i], pool_scale[i], ts=ts_ab)
        else:
            p = dict(mu=rwkv_mu[i], w_r=rwkv_w_r[i], w_k=rwkv_w_k[i], w_v=rwkv_w_v[i],
                     w0=rwkv_w0[i], w1=rwkv_w1[i], w2=rwkv_w2[i], a0=rwkv_a0[i], a1=rwkv_a1[i],
                     a2=rwkv_a2[i], g1=rwkv_g1[i], g2=rwkv_g2[i], k_k=rwkv_k_k[i],
                     k_a=rwkv_k_a[i], r_k=rwkv_r_k[i])
            vres = None if i == 0 else (rwkv_v0[i - 1], rwkv_v1[i - 1], rwkv_v2[i - 1])
            scan_in, gate, bonus, v_first = _rwkv_pre(h, mix_norm[layer], p, v_first, vres,
                                                       ts=ts_pre)
            y = _wkv(scan_in, npair=8, nchunk=2)
            post = (y, gate, bonus, rwkv_ln_w[i], rwkv_ln_b[i], rwkv_w_o[i])
        h = _ffn(h, ffn_norm[layer], ffn_w_gate[layer], ffn_w_up[layer], ffn_w_down[layer],
                 tm=tm, post=post, final_norm=final_norm if layer == depth - 1 else None)
    return h
```

```python
import functools
import math

import jax
import jax.numpy as jnp
from jax import lax
from jax.experimental import pallas as pl
from jax.experimental.pallas import tpu as pltpu

F32 = jnp.float32
BF16 = jnp.bfloat16

EPS = 1e-6
GN_EPS = 64e-5
LANES = 128
HEAD_DIM = 64
WKV_CHUNK = 64
GMLP_BLOCK = 128
GMLP_CAUSAL_CHUNK = 64
POOL_WINDOWS = (2, 4, 8, 16)
POOL_HALO = 16
VMEM_LIMIT = 56 * 1024 * 1024


def _const_spec(shape):
    nd = len(shape)
    return pl.BlockSpec(shape, lambda *_: (0,) * nd, pipeline_mode=pl.Buffered(1))


def _rms(x, g):
    return x * lax.rsqrt(jnp.mean(x * x, axis=-1, keepdims=True) + EPS) * g


def _bdot(a, b):
    return jnp.dot(a.astype(BF16), b.astype(BF16), preferred_element_type=F32)


def _head_sum(x):
    lane_lo = lax.broadcasted_iota(jnp.int32, (1, LANES), 1) < HEAD_DIM
    outs = []
    for j in range(x.shape[1] // LANES):
        blk = x[:, j * LANES:(j + 1) * LANES]
        lo = jnp.sum(jnp.where(lane_lo, blk, 0.0), axis=-1, keepdims=True)
        tot = jnp.sum(blk, axis=-1, keepdims=True)
        outs.append(jnp.where(lane_lo, lo, tot - lo))
    return jnp.concatenate(outs, axis=1)


def _ab_kernel(x_ref, gmix_ref, win_ref, wout_ref, sgain_ref, ws_ref, sbias_ref, pw_ref,
               pscale_ref, o_ref, zs_ref, *, ts, a_width):
    s = pl.program_id(1)
    h = x_ref[...]
    hn = _rms(h, gmix_ref[...])
    z = _bdot(hn, win_ref[...])
    zg = z[:, :2 * a_width]
    za = 0.5 * zg * (1.0 + lax.erf(zg * (1.0 / math.sqrt(2.0))))
    u = za[:, :a_width]
    vn = _rms(za[:, a_width:], sgain_ref[...]).astype(BF16)

    ri = lax.broadcasted_iota(jnp.int32, (GMLP_BLOCK, GMLP_BLOCK), 0) // GMLP_CAUSAL_CHUNK
    ci = lax.broadcasted_iota(jnp.int32, (GMLP_BLOCK, GMLP_BLOCK), 1) // GMLP_CAUSAL_CHUNK
    keep = ri >= ci
    heads = []
    for hd in range(a_width // LANES):
        cs = slice(hd * LANES, (hd + 1) * LANES)
        w = jnp.where(keep, ws_ref[hd], 0.0).astype(BF16)
        bias = sbias_ref[:, cs]
        rows = []
        for blk in range(ts // GMLP_BLOCK):
            vb = vn[blk * GMLP_BLOCK:(blk + 1) * GMLP_BLOCK, cs]
            rows.append(jnp.dot(w, vb, preferred_element_type=F32) + bias)
        heads.append(jnp.concatenate(rows, axis=0))
    y_a = u * jnp.concatenate(heads, axis=1)

    zb = z[:, 2 * a_width:]

    @pl.when(s == 0)
    def _():
        zs_ref[0:POOL_HALO, :] = jnp.zeros((POOL_HALO, zb.shape[1]), F32)

    zs_ref[POOL_HALO:POOL_HALO + ts, :] = zb
    t1 = (lax.broadcasted_iota(jnp.int32, (ts, 1), 0) + s * ts + 1).astype(F32)
    groups = []
    for gi, win in enumerate(POOL_WINDOWS):
        cs = slice(gi * LANES, (gi + 1) * LANES)
        cur = zb[:, cs]
        acc = cur
        for j in range(1, win):
            acc = acc + zs_ref[pl.ds(POOL_HALO - j, ts), cs]
        pooled = acc / jnp.minimum(t1, float(win)) - cur
        groups.append(_bdot(pooled, pw_ref[gi]))
    y_b = jnp.concatenate(groups, axis=1) * pscale_ref[...]
    zs_ref[0:POOL_HALO, :] = zs_ref[ts:ts + POOL_HALO, :]

    cat = jnp.concatenate([y_a, y_b], axis=1)
    o_ref[...] = h + _bdot(cat, wout_ref[...])


def _ab_mixer(h, gmix, w_in, w_out, sgain, w_s, s_bias, pool_w, pool_scale, *, ts):
    bsz, seq, d = h.shape
    a_width = sgain.shape[0]
    b_width = pool_scale.shape[0]
    n_heads = w_s.shape[0]
    bias_full = jnp.repeat(s_bias.T, a_width // n_heads, axis=1)
    row = lambda v: v.reshape(1, -1)
    tok = pl.BlockSpec((None, ts, d), lambda b, s: (b, s, 0))
    return pl.pallas_call(
        functools.partial(_ab_kernel, ts=ts, a_width=a_width),
        grid=(bsz, seq // ts),
        in_specs=[tok, _const_spec((1, d)), _const_spec(w_in.shape), _const_spec(w_out.shape),
                  _const_spec((1, a_width)), _const_spec(w_s.shape), _const_spec(bias_full.shape),
                  _const_spec(pool_w.shape), _const_spec((1, b_width))],
        out_specs=tok,
        out_shape=jax.ShapeDtypeStruct(h.shape, F32),
        scratch_shapes=[pltpu.VMEM((ts + POOL_HALO, b_width), F32)],
        compiler_params=pltpu.CompilerParams(
            dimension_semantics=("arbitrary", "arbitrary"), vmem_limit_bytes=VMEM_LIMIT),
        name="ab_mixer",
    )(h, row(gmix), w_in.astype(BF16), w_out.astype(BF16), row(sgain), w_s, bias_full,
      pool_w.astype(BF16), row(pool_scale))


def _swiglu(h, gn_ref, wg_ref, wu_ref, wd_ref, fin_ref, between=()):
    between = list(between) + [None] * 3
    hn = _rms(h, gn_ref[...]).astype(BF16)
    gate = jnp.dot(hn, wg_ref[...], preferred_element_type=F32)
    if between[0]:
        between[0]()
    up = jnp.dot(hn, wu_ref[...], preferred_element_type=F32)
    if between[1]:
        between[1]()
    mid = gate * jax.nn.sigmoid(gate) * up
    out = h + _bdot(mid, wd_ref[...])
    if between[2]:
        between[2]()
    if fin_ref is not None:
        out = _rms(out, fin_ref[...])
    return out


def _ffn_kernel(x_ref, gn_ref, wg_ref, wu_ref, wd_ref, *rest):
    *fin, o_ref = rest
    o_ref[...] = _swiglu(x_ref[...], gn_ref, wg_ref, wu_ref, wd_ref, fin[0] if fin else None)


def _ffn_post_kernel(x_ref, y_ref, g_ref, bonus_ref, lnw_ref, lnb_ref, wo_ref, gn_ref, wg_ref,
                     wu_ref, wd_ref, *rest, tm):
    *fin, o_ref, buf_a, buf_b = rest
    s = pl.program_id(0)

    @pl.when(s == 0)
    def _():
        buf_b[...] = jnp.zeros(buf_b.shape, BF16)

    def prepare(wr, rs):
        y = y_ref[rs, :]
        mean = _head_sum(y) * (1.0 / HEAD_DIM)
        dev = y - mean
        var = _head_sum(dev * dev) * (1.0 / HEAD_DIM)
        yn = dev * lax.rsqrt(var + GN_EPS) * lnw_ref[...] + lnb_ref[...]
        wr[rs, :] = ((yn + bonus_ref[rs, :]) * g_ref[rs, :]).astype(BF16)

    def step(rd, wr):
        quarter = tm // 4
        pieces = [functools.partial(prepare, wr, slice(q * quarter, (q + 1) * quarter))
                  for q in range(4)]
        h = x_ref[...] + jnp.dot(rd[...], wo_ref[...], preferred_element_type=F32)
        pieces[0]()
        o_ref[...] = _swiglu(h, gn_ref, wg_ref, wu_ref, wd_ref, fin[0] if fin else None,
                             between=pieces[1:])

    @pl.when(s % 2 == 0)
    def _():
        step(buf_b, buf_a)

    @pl.when(s % 2 == 1)
    def _():
        step(buf_a, buf_b)


def _ffn(h, gnorm, w_gate, w_up, w_down, *, tm, post=None, final_norm=None):
    bsz, seq, d = h.shape
    n = bsz * seq
    nblk = n // tm
    row = lambda v: v.reshape(1, -1)
    ffn_args = [row(gnorm), w_gate.astype(BF16), w_up.astype(BF16), w_down.astype(BF16)]
    if final_norm is not None:
        ffn_args.append(row(final_norm))
    ffn_specs = [_const_spec(a.shape) for a in ffn_args]
    params = pltpu.CompilerParams(dimension_semantics=("arbitrary",), vmem_limit_bytes=VMEM_LIMIT)
    out_shape = jax.ShapeDtypeStruct((n, d), F32)
    if post is None:
        tok = pl.BlockSpec((tm, d), lambda i: (i, 0))
        out = pl.pallas_call(
            _ffn_kernel, grid=(nblk,), in_specs=[tok] + ffn_specs, out_specs=tok,
            out_shape=out_shape, compiler_params=params, name="ffn",
        )(h.reshape(n, d), *ffn_args)
    else:
        y, g, bonus, ln_w, ln_b, w_o = post
        cur = pl.BlockSpec((tm, d), lambda i: (jnp.minimum(i, nblk - 1), 0))
        lag = pl.BlockSpec((tm, d), lambda i: (jnp.maximum(i - 1, 0), 0))
        out = pl.pallas_call(
            functools.partial(_ffn_post_kernel, tm=tm), grid=(nblk + 1,),
            in_specs=[lag, cur, cur, cur, _const_spec((1, d)), _const_spec((1, d)),
                      _const_spec(w_o.shape)] + ffn_specs,
            out_specs=lag, out_shape=out_shape,
            scratch_shapes=[pltpu.VMEM((tm, d), BF16), pltpu.VMEM((tm, d), BF16)],
            compiler_params=params, name="ffn_post",
        )(h.reshape(n, d), y.reshape(n, d), g.reshape(n, d), bonus.reshape(n, d), row(ln_w),
          row(ln_b), w_o.astype(BF16), *ffn_args)
    return out.reshape(bsz, seq, d)


def _sigmoid(x):
    return 0.5 * jnp.tanh(0.5 * x) + 0.5


def _rwkv_pre_kernel(*refs, ts, vres):
    it = iter(refs)
    (x_ref, gmix_ref, mu_ref, wr_ref, wk_ref, wv_ref, w0_ref, w1_ref, w2_ref, a0_ref, a1_ref,
     a2_ref, g1_ref, g2_ref, kk_ref, ka_ref, rk_ref) = (next(it) for _ in range(17))
    if vres:
        vf_ref, v0_ref, v1_ref, v2_ref = (next(it) for _ in range(4))
    (r_ref, k_ref, nkk_ref, kka_ref, vb_ref, lw_ref, cum_ref, g_ref,
     bonus_ref) = (next(it) for _ in range(9))
    if not vres:
        vf_out_ref = next(it)
    prev_ref, buf_a, buf_b = next(it), next(it), next(it)

    s = pl.program_id(1)

    @pl.when(s == 0)
    def _():
        prev_ref[...] = jnp.zeros(prev_ref.shape, F32)
        buf_b[...] = jnp.zeros(buf_b.shape, F32)

    dot = lambda a, b_ref: jnp.dot(a, b_ref[...], preferred_element_type=F32)

    def step(rd, wr):
        hn = _rms(x_ref[...], gmix_ref[...])
        first = lax.broadcasted_iota(jnp.int32, (ts, 1), 0) == 0
        hprev = jnp.where(first, prev_ref[0:1, :], pltpu.roll(hn, 1, axis=0))
        prev_ref[0:1, :] = hn[ts - 1:ts, :]
        hn_b = hn.astype(BF16)
        xx_b = (hprev - hn).astype(BF16)
        mu_b = mu_ref[...].astype(BF16)
        xr, xw, xk, xv, xa, xg = (hn_b + xx_b * mu_b[i:i + 1, :] for i in range(6))
        hw, ha, hg = dot(xw, w1_ref), dot(xa, a1_ref), dot(xg, g1_ref)
        if vres:
            hv = dot(xv, v1_ref)

        lw = -math.exp(-0.5) * _sigmoid(w0_ref[...] + rd[3])
        lw_ref[...] = lw
        hi = lw.astype(BF16)
        lo = (lw - hi.astype(F32)).astype(BF16)
        ri = lax.broadcasted_iota(jnp.int32, (ts, ts), 0)
        ci = lax.broadcasted_iota(jnp.int32, (ts, ts), 1)
        tri = jnp.where((ri // WKV_CHUNK == ci // WKV_CHUNK) & (ri >= ci), 1.0, 0.0).astype(BF16)

        def tail(rs):
            r, k, v = rd[0, rs, :], rd[1, rs, :], rd[2, rs, :]
            a = _sigmoid(a0_ref[...] + rd[4, rs, :])
            g_ref[rs, :] = rd[5, rs, :]
            if vres:
                v = v + (vf_ref[rs, :] - v) * _sigmoid(v0_ref[...] + rd[6, rs, :])
            else:
                vf_out_ref[rs, :] = v
            vb_ref[rs, :] = v.astype(BF16)
            kk = k * kk_ref[...]
            kk = kk * lax.rsqrt(jnp.maximum(_head_sum(kk * kk), 1e-24))
            k = k * (1.0 + (a - 1.0) * ka_ref[...])
            bonus_ref[rs, :] = _head_sum(r * k * rk_ref[...]) * v
            r_ref[rs, :] = r.astype(BF16)
            k_ref[rs, :] = k.astype(BF16)
            nkk_ref[rs, :] = (-kk).astype(BF16)
            kka_ref[rs, :] = (kk * a).astype(BF16)

        quarter = ts // 4
        wr[0] = dot(xr, wr_ref)
        c2 = jnp.dot(tri, jnp.concatenate([hi, lo], axis=1), preferred_element_type=F32)
        d = lw.shape[1]
        cum_ref[...] = c2[:, :d] + c2[:, d:]
        tail(slice(0, quarter))
        wr[1] = dot(xk, wk_ref)
        tail(slice(quarter, 2 * quarter))
        wr[2] = dot(xv, wv_ref)
        tail(slice(2 * quarter, 3 * quarter))
        wr[3] = dot(jnp.tanh(hw).astype(BF16), w2_ref)
        wr[4] = dot(ha.astype(BF16), a2_ref)
        wr[5] = dot(_sigmoid(hg).astype(BF16), g2_ref)
        if vres:
            wr[6] = dot(hv.astype(BF16), v2_ref)
        tail(slice(3 * quarter, ts))

    @pl.when(s % 2 == 0)
    def _():
        step(buf_b, buf_a)

    @pl.when(s % 2 == 1)
    def _():
        step(buf_a, buf_b)


def _rwkv_pre(h, gmix, p, v_first, vres, *, ts):
    bsz, seq, d = h.shape
    nblk = seq // ts
    row = lambda v: v.reshape(1, -1)
    cur = pl.BlockSpec((None, ts, d), lambda b, s: (b, jnp.minimum(s, nblk - 1), 0))
    lag = pl.BlockSpec((None, ts, d), lambda b, s: (b, jnp.maximum(s - 1, 0), 0))
    wb = lambda n: p[n].astype(BF16)
    args = [h, row(gmix), p["mu"], wb("w_r"), wb("w_k"), wb("w_v"), row(p["w0"]), wb("w1"),
            wb("w2"), row(p["a0"]), wb("a1"), wb("a2"), wb("g1"), wb("g2"), row(p["k_k"]),
            row(p["k_a"]), row(p["r_k"])]
    specs = [cur] + [_const_spec(a.shape) for a in args[1:]]
    if vres is not None:
        v0, v1, v2 = vres
        extra = [row(v0), v1.astype(BF16), v2.astype(BF16)]
        args += [v_first] + extra
        specs += [lag] + [_const_spec(a.shape) for a in extra]
    tok_bf = jax.ShapeDtypeStruct(h.shape, BF16)
    tok_f32 = jax.ShapeDtypeStruct(h.shape, F32)
    out_shape = [tok_bf] * 5 + [tok_f32] * (4 if vres is not None else 5)
    nproj = 7 if vres is not None else 6
    outs = pl.pallas_call(
        functools.partial(_rwkv_pre_kernel, ts=ts, vres=vres is not None),
        grid=(bsz, nblk + 1),
        in_specs=specs,
        out_specs=[lag] * len(out_shape),
        out_shape=out_shape,
        scratch_shapes=[pltpu.VMEM((8, d), F32), pltpu.VMEM((nproj, ts, d), F32),
                        pltpu.VMEM((nproj, ts, d), F32)],
        compiler_params=pltpu.CompilerParams(
            dimension_semantics=("arbitrary", "arbitrary"), vmem_limit_bytes=VMEM_LIMIT),
        name="rwkv_pre_vres" if vres is not None else "rwkv_pre",
    )(*args)
    scan_in, gate, bonus = outs[:7], outs[7], outs[8]
    if vres is None:
        v_first = outs[9]
    return scan_in, gate, bonus, v_first


def _wkv_kernel(r_ref, k_ref, nkk_ref, kka_ref, v_ref, lw_ref, cum_ref, y_ref, s_ref,
                *, npair, nchunk):
    ch = WKV_CHUNK
    pair = 2 * ch

    @pl.when(pl.program_id(2) == 0)
    def _():
        s_ref[...] = jnp.zeros(s_ref.shape, F32)

    row = lax.broadcasted_iota(jnp.int32, (pair, pair), 0)
    col = lax.broadcasted_iota(jnp.int32, (pair, pair), 1)
    same = (row // ch) == (col // ch)
    same_bf = jnp.where(same, 1.0, 0.0).astype(BF16)
    trow = lax.broadcasted_iota(jnp.int32, (ch, pair), 0)
    tcol = lax.broadcasted_iota(jnp.int32, (ch, pair), 1) % ch
    strict = trow > tcol
    incl = trow >= tcol
    eye = jnp.where(trow == tcol, 1.0, 0.0).astype(F32)
    nt = (((1,), (1,)), ((), ()))
    tn = (((0,), (0,)), ((), ()))
    dot = functools.partial(jnp.dot, preferred_element_type=F32)

    def stack(x):
        return jnp.concatenate([x, x], axis=0) * same_bf

    items = [(ck, p) for ck in range(nchunk) for p in range(npair)]
    sl = {(ck, p): (slice(ck * ch, (ck + 1) * ch), slice(p * LANES, (p + 1) * LANES))
          for ck, p in items}
    ar_lhs, v_bd, a_ak, a_rbk, power, tinv, bk_end, g_end = {}, {}, {}, {}, {}, {}, {}, {}
    for it in items:
        rs, ls = sl[it]
        cum = cum_ref[rs, ls]
        g_end[it] = jnp.exp(cum[ch - 1:ch, :])
        inv = jnp.exp(-cum)
        inv_b = inv.astype(BF16)
        end_b = (inv * g_end[it]).astype(BF16)
        k, kka = k_ref[rs, ls], kka_ref[rs, ls]
        bk_end[it] = jnp.concatenate([kka * end_b, k * end_b], axis=0)
        v_bd[it] = stack(v_ref[rs, ls])
        ar_lhs[it] = jnp.concatenate(
            [nkk_ref[rs, ls] * jnp.exp(cum - lw_ref[rs, ls]).astype(BF16),
             r_ref[rs, ls] * jnp.exp(cum).astype(BF16)], axis=0)
        rhs = jnp.concatenate([stack(kka * inv_b), stack(k * inv_b)], axis=0)
        amat = lax.dot_general(ar_lhs[it], rhs, nt, preferred_element_type=F32)
        power[it] = jnp.where(strict, amat[:ch, :pair], 0.0)
        tinv[it] = eye + power[it]
        a_ak[it] = jnp.where(strict, amat[:ch, pair:], 0.0).astype(BF16)
        a_rbk[it] = jnp.where(jnp.concatenate([incl, incl], axis=1), amat[ch:], 0.0).astype(BF16)

    for it in items:
        pb = power[it].astype(BF16)
        power[it] = dot(pb, stack(pb))
    for _ in range(int(math.log2(ch)) - 2):
        for it in items:
            pb = power[it].astype(BF16)
            both = dot(jnp.concatenate([pb, tinv[it].astype(BF16)], axis=0), stack(pb))
            power[it] = both[:ch]
            tinv[it] = tinv[it] + both[ch:]
    for it in items:
        tinv[it] = (tinv[it] + dot(tinv[it].astype(BF16),
                                   stack(power[it].astype(BF16)))).astype(BF16)
    xv = {it: dot(a_ak[it], v_bd[it]) for it in items}

    state = [s_ref[p] for p in range(npair)]
    for ck in range(nchunk):
        ar, u = {}, {}
        for p in range(npair):
            ar[p] = lax.dot_general(ar_lhs[ck, p], state[p].astype(BF16), nt,
                                    preferred_element_type=F32)
        for p in range(npair):
            x = (ar[p][:ch] + xv[ck, p]).astype(BF16)
            u[p] = dot(tinv[ck, p], stack(x)).astype(BF16)
        for p in range(npair):
            rs, ls = sl[ck, p]
            uv_bd = jnp.concatenate([stack(u[p]), v_bd[ck, p]], axis=0)
            y_ref[rs, ls] = ar[p][ch:] + dot(a_rbk[ck, p], uv_bd)
            upd = lax.dot_general(jnp.concatenate([u[p], v_ref[rs, ls]], axis=0), bk_end[ck, p],
                                  tn, preferred_element_type=F32)
            state[p] = state[p] * g_end[ck, p] + jnp.where(same, upd, 0.0)
    for p in range(npair):
        s_ref[p] = state[p]


def _wkv(scan_in, *, npair, nchunk):
    rt = scan_in[0]
    bsz, seq, d = rt.shape
    width = npair * LANES
    rows = nchunk * WKV_CHUNK
    tok = pl.BlockSpec((None, rows, width), lambda b, p, c: (b, c, p))
    return pl.pallas_call(
        functools.partial(_wkv_kernel, npair=npair, nchunk=nchunk),
        grid=(bsz, d // width, seq // rows),
        in_specs=[tok] * 7,
        out_specs=tok,
        out_shape=jax.ShapeDtypeStruct((bsz, seq, d), F32),
        scratch_shapes=[pltpu.VMEM((npair, LANES, LANES), F32)],
        compiler_params=pltpu.CompilerParams(
            dimension_semantics=("arbitrary", "arbitrary", "arbitrary"),
            vmem_limit_bytes=VMEM_LIMIT),
        name="wkv",
    )(*scan_in)


def kernel(x, mix_norm, ffn_norm, final_norm, ab_w_in, ab_w_out, sgu_gain, sgu_w_s, sgu_bias, pool_w, pool_scale, rwkv_mu, rwkv_w_r, rwkv_w_k, rwkv_w_v, rwkv_w_o, rwkv_w0, rwkv_w1, rwkv_w2, rwkv_a0, rwkv_a1, rwkv_a2, rwkv_g1, rwkv_g2, rwkv_k_k, rwkv_k_a, rwkv_r_k, rwkv_ln_w, rwkv_ln_b, rwkv_v0, rwkv_v1, rwkv_v2, ffn_w_gate, ffn_w_up, ffn_w_down):
    depth = mix_norm.shape[0]
    seq = x.shape[1]
    ts_ab = min(512, seq)
    ts_pre = min(256, seq)
    tm = min(512, seq)
    h = x
    v_first = None
    for layer in range(depth):
        i = layer // 2
        post = None
        if layer % 2 == 0:
            h = _ab_mixer(h, mix_norm[layer], ab_w_in[i], ab_w_out[i], sgu_gain[i], sgu_w_s[i],
                          sgu_bias[i], pool_w[i], pool_scale[i], ts=ts_ab)
        else:
            p = dict(mu=rwkv_mu[i], w_r=rwkv_w_r[i], w_k=rwkv_w_k[i], w_v=rwkv_w_v[i],
                     w0=rwkv_w0[i], w1=rwkv_w1[i], w2=rwkv_w2[i], a0=rwkv_a0[i], a1=rwkv_a1[i],
                     a2=rwkv_a2[i], g1=rwkv_g1[i], g2=rwkv_g2[i], k_k=rwkv_k_k[i],
                     k_a=rwkv_k_a[i], r_k=rwkv_r_k[i])
            vres = None if i == 0 else (rwkv_v0[i - 1], rwkv_v1[i - 1], rwkv_v2[i - 1])
            scan_in, gate, bonus, v_first = _rwkv_pre(h, mix_norm[layer], p, v_first, vres,
                                                       ts=ts_pre)
            y = _wkv(scan_in, npair=8, nchunk=4)
            post = (y, gate, bonus, rwkv_ln_w[i], rwkv_ln_b[i], rwkv_w_o[i])
        h = _ffn(h, ffn_norm[layer], ffn_w_gate[layer], ffn_w_up[layer], ffn_w_down[layer],
                 tm=tm, post=post, final_norm=final_norm if layer == depth - 1 else None)
    return h
```

```python
import functools
import math

import jax
import jax.numpy as jnp
from jax import lax
from jax.experimental import pallas as pl
from jax.experimental.pallas import tpu as pltpu

F32 = jnp.float32
BF16 = jnp.bfloat16

EPS = 1e-6
GN_EPS = 64e-5
LANES = 128
HEAD_DIM = 64
WKV_CHUNK = 64
GMLP_BLOCK = 128
GMLP_CAUSAL_CHUNK = 64
POOL_WINDOWS = (2, 4, 8, 16)
POOL_HALO = 16
VMEM_LIMIT = 56 * 1024 * 1024


def _const_spec(shape):
    nd = len(shape)
    return pl.BlockSpec(shape, lambda *_: (0,) * nd, pipeline_mode=pl.Buffered(1))


def _rms(x, g):
    return x * lax.rsqrt(jnp.mean(x * x, axis=-1, keepdims=True) + EPS) * g


def _bdot(a, b):
    return jnp.dot(a.astype(BF16), b.astype(BF16), preferred_element_type=F32)


def _head_sum(x):
    lane_lo = lax.broadcasted_iota(jnp.int32, (1, LANES), 1) < HEAD_DIM
    outs = []
    for j in range(x.shape[1] // LANES):
        blk = x[:, j * LANES:(j + 1) * LANES]
        lo = jnp.sum(jnp.where(lane_lo, blk, 0.0), axis=-1, keepdims=True)
        tot = jnp.sum(blk, axis=-1, keepdims=True)
        outs.append(jnp.where(lane_lo, lo, tot - lo))
    return jnp.concatenate(outs, axis=1)


def _ab_kernel(x_ref, gmix_ref, win_ref, wout_ref, sgain_ref, ws_ref, sbias_ref, pw_ref,
               pscale_ref, o_ref, zs_ref, *, ts, a_width):
    s = pl.program_id(1)
    half = ts // 2
    dot = functools.partial(jnp.dot, preferred_element_type=F32)
    gelu = lambda z: 0.5 * z * (1.0 + lax.erf(z * (1.0 / math.sqrt(2.0))))

    @pl.when(s == 0)
    def _():
        zs_ref[0:POOL_HALO, :] = jnp.zeros((POOL_HALO, zs_ref.shape[1]), F32)

    ri = lax.broadcasted_iota(jnp.int32, (GMLP_BLOCK, GMLP_BLOCK), 0) // GMLP_CAUSAL_CHUNK
    ci = lax.broadcasted_iota(jnp.int32, (GMLP_BLOCK, GMLP_BLOCK), 1) // GMLP_CAUSAL_CHUNK
    keep = ri >= ci
    w_s = [jnp.where(keep, ws_ref[hd], 0.0).astype(BF16) for hd in range(a_width // LANES)]

    def in_proj(r0):
        h = x_ref[r0:r0 + half, :]
        hn = _rms(h, gmix_ref[...]).astype(BF16)
        return (h, dot(hn, win_ref[:, :a_width]), dot(hn, win_ref[:, a_width:2 * a_width]),
                dot(hn, win_ref[:, 2 * a_width:]))

    def gate_inputs(zu, zv):
        return gelu(zu), _rms(gelu(zv), sgain_ref[...]).astype(BF16)

    def spatial_gate(u, vn):
        heads = []
        for hd, w in enumerate(w_s):
            cs = slice(hd * LANES, (hd + 1) * LANES)
            bias = sbias_ref[:, cs]
            rows = [dot(w, vn[b0:b0 + GMLP_BLOCK, cs]) + bias
                    for b0 in range(0, half, GMLP_BLOCK)]
            heads.append(jnp.concatenate(rows, axis=0))
        return u * jnp.concatenate(heads, axis=1)

    def pool(r0, zb):
        base = POOL_HALO + r0
        zs_ref[base:base + half, :] = zb
        t1 = (lax.broadcasted_iota(jnp.int32, (half, 1), 0) + (s * ts + r0 + 1)).astype(F32)
        groups = []
        for gi, win in enumerate(POOL_WINDOWS):
            cs = slice(gi * LANES, (gi + 1) * LANES)
            cur = zb[:, cs]
            acc = cur
            for j in range(1, win):
                acc = acc + zs_ref[pl.ds(base - j, half), cs]
            pooled = acc / jnp.minimum(t1, float(win)) - cur
            groups.append(_bdot(pooled, pw_ref[gi]))
        return jnp.concatenate(groups, axis=1) * pscale_ref[...]

    def out_proj(r0, h, y_a, y_b):
        cat = jnp.concatenate([y_a, y_b], axis=1)
        o_ref[r0:r0 + half, :] = h + _bdot(cat, wout_ref[...])

    h0, zu0, zv0, zp0 = in_proj(0)
    h1, zu1, zv1, zp1 = in_proj(half)
    u0, vn0 = gate_inputs(zu0, zv0)
    ya0 = spatial_gate(u0, vn0)
    yb0 = pool(0, zp0)
    u1, vn1 = gate_inputs(zu1, zv1)
    out_proj(0, h0, ya0, yb0)
    ya1 = spatial_gate(u1, vn1)
    yb1 = pool(half, zp1)
    out_proj(half, h1, ya1, yb1)
    zs_ref[0:POOL_HALO, :] = zs_ref[ts:ts + POOL_HALO, :]


def _ab_mixer(h, gmix, w_in, w_out, sgain, w_s, s_bias, pool_w, pool_scale, *, ts):
    bsz, seq, d = h.shape
    a_width = sgain.shape[0]
    b_width = pool_scale.shape[0]
    n_heads = w_s.shape[0]
    bias_full = jnp.repeat(s_bias.T, a_width // n_heads, axis=1)
    row = lambda v: v.reshape(1, -1)
    tok = pl.BlockSpec((None, ts, d), lambda b, s: (b, s, 0))
    return pl.pallas_call(
        functools.partial(_ab_kernel, ts=ts, a_width=a_width),
        grid=(bsz, seq // ts),
        in_specs=[tok, _const_spec((1, d)), _const_spec(w_in.shape), _const_spec(w_out.shape),
                  _const_spec((1, a_width)), _const_spec(w_s.shape), _const_spec(bias_full.shape),
                  _const_spec(pool_w.shape), _const_spec((1, b_width))],
        out_specs=tok,
        out_shape=jax.ShapeDtypeStruct(h.shape, F32),
        scratch_shapes=[pltpu.VMEM((ts + POOL_HALO, b_width), F32)],
        compiler_params=pltpu.CompilerParams(
            dimension_semantics=("arbitrary", "arbitrary"), vmem_limit_bytes=VMEM_LIMIT),
        name="ab_mixer",
    )(h, row(gmix), w_in.astype(BF16), w_out.astype(BF16), row(sgain), w_s, bias_full,
      pool_w.astype(BF16), row(pool_scale))


def _swiglu(h, gn_ref, wg_ref, wu_ref, wd_ref, fin_ref, between=()):
    between = list(between) + [None] * 3
    hn = _rms(h, gn_ref[...]).astype(BF16)
    gate = jnp.dot(hn, wg_ref[...], preferred_element_type=F32)
    if between[0]:
        between[0]()
    up = jnp.dot(hn, wu_ref[...], preferred_element_type=F32)
    if between[1]:
        between[1]()
    mid = gate * jax.nn.sigmoid(gate) * up
    out = h + _bdot(mid, wd_ref[...])
    if between[2]:
        between[2]()
    if fin_ref is not None:
        out = _rms(out, fin_ref[...])
    return out


def _ffn_kernel(x_ref, gn_ref, wg_ref, wu_ref, wd_ref, *rest):
    *fin, o_ref = rest
    o_ref[...] = _swiglu(x_ref[...], gn_ref, wg_ref, wu_ref, wd_ref, fin[0] if fin else None)


def _ffn_post_kernel(x_ref, y_ref, g_ref, bonus_ref, lnw_ref, lnb_ref, wo_ref, gn_ref, wg_ref,
                     wu_ref, wd_ref, *rest, tm):
    *fin, o_ref, buf_a, buf_b = rest
    s = pl.program_id(0)

    @pl.when(s == 0)
    def _():
        buf_b[...] = jnp.zeros(buf_b.shape, BF16)

    def prepare(wr, rs):
        y = y_ref[rs, :]
        mean = _head_sum(y) * (1.0 / HEAD_DIM)
        dev = y - mean
        var = _head_sum(dev * dev) * (1.0 / HEAD_DIM)
        yn = dev * lax.rsqrt(var + GN_EPS) * lnw_ref[...] + lnb_ref[...]
        wr[rs, :] = ((yn + bonus_ref[rs, :]) * g_ref[rs, :]).astype(BF16)

    def step(rd, wr):
        quarter = tm // 4
        pieces = [functools.partial(prepare, wr, slice(q * quarter, (q + 1) * quarter))
                  for q in range(4)]
        h = x_ref[...] + jnp.dot(rd[...], wo_ref[...], preferred_element_type=F32)
        pieces[0]()
        o_ref[...] = _swiglu(h, gn_ref, wg_ref, wu_ref, wd_ref, fin[0] if fin else None,
                             between=pieces[1:])

    @pl.when(s % 2 == 0)
    def _():
        step(buf_b, buf_a)

    @pl.when(s % 2 == 1)
    def _():
        step(buf_a, buf_b)


def _ffn(h, gnorm, w_gate, w_up, w_down, *, tm, post=None, final_norm=None):
    bsz, seq, d = h.shape
    n = bsz * seq
    nblk = n // tm
    row = lambda v: v.reshape(1, -1)
    ffn_args = [row(gnorm), w_gate.astype(BF16), w_up.astype(BF16), w_down.astype(BF16)]
    if final_norm is not None:
        ffn_args.append(row(final_norm))
    ffn_specs = [_const_spec(a.shape) for a in ffn_args]
    params = pltpu.CompilerParams(dimension_semantics=("arbitrary",), vmem_limit_bytes=VMEM_LIMIT)
    out_shape = jax.ShapeDtypeStruct((n, d), F32)
    if post is None:
        tok = pl.BlockSpec((tm, d), lambda i: (i, 0))
        out = pl.pallas_call(
            _ffn_kernel, grid=(nblk,), in_specs=[tok] + ffn_specs, out_specs=tok,
            out_shape=out_shape, compiler_params=params, name="ffn",
        )(h.reshape(n, d), *ffn_args)
    else:
        y, g, bonus, ln_w, ln_b, w_o = post
        cur = pl.BlockSpec((tm, d), lambda i: (jnp.minimum(i, nblk - 1), 0))
        lag = pl.BlockSpec((tm, d), lambda i: (jnp.maximum(i - 1, 0), 0))
        out = pl.pallas_call(
            functools.partial(_ffn_post_kernel, tm=tm), grid=(nblk + 1,),
            in_specs=[lag, cur, cur, cur, _const_spec((1, d)), _const_spec((1, d)),
                      _const_spec(w_o.shape)] + ffn_specs,
            out_specs=lag, out_shape=out_shape,
            scratch_shapes=[pltpu.VMEM((tm, d), BF16), pltpu.VMEM((tm, d), BF16)],
            compiler_params=params, name="ffn_post",
        )(h.reshape(n, d), y.reshape(n, d), g.reshape(n, d), bonus.reshape(n, d), row(ln_w),
          row(ln_b), w_o.astype(BF16), *ffn_args)
    return out.reshape(bsz, seq, d)


def _sigmoid(x):
    return 0.5 * jnp.tanh(0.5 * x) + 0.5


def _rwkv_pre_kernel(*refs, ts, vres):
    it = iter(refs)
    (x_ref, gmix_ref, mu_ref, wr_ref, wk_ref, wv_ref, w0_ref, w1_ref, w2_ref, a0_ref, a1_ref,
     a2_ref, g1_ref, g2_ref, kk_ref, ka_ref, rk_ref) = (next(it) for _ in range(17))
    if vres:
        vf_ref, v0_ref, v1_ref, v2_ref = (next(it) for _ in range(4))
    (r_ref, k_ref, nkk_ref, kka_ref, vb_ref, lw_ref, cum_ref, g_ref,
     bonus_ref) = (next(it) for _ in range(9))
    if not vres:
        vf_out_ref = next(it)
    prev_ref, buf_a, buf_b = next(it), next(it), next(it)

    s = pl.program_id(1)

    @pl.when(s == 0)
    def _():
        prev_ref[...] = jnp.zeros(prev_ref.shape, F32)
        buf_b[...] = jnp.zeros(buf_b.shape, F32)

    dot = lambda a, b_ref: jnp.dot(a, b_ref[...], preferred_element_type=F32)

    def step(rd, wr):
        hn = _rms(x_ref[...], gmix_ref[...])
        first = lax.broadcasted_iota(jnp.int32, (ts, 1), 0) == 0
        hprev = jnp.where(first, prev_ref[0:1, :], pltpu.roll(hn, 1, axis=0))
        prev_ref[0:1, :] = hn[ts - 1:ts, :]
        hn_b = hn.astype(BF16)
        xx_b = (hprev - hn).astype(BF16)
        mu_b = mu_ref[...].astype(BF16)
        xr, xw, xk, xv, xa, xg = (hn_b + xx_b * mu_b[i:i + 1, :] for i in range(6))
        hw, ha, hg = dot(xw, w1_ref), dot(xa, a1_ref), dot(xg, g1_ref)
        if vres:
            hv = dot(xv, v1_ref)

        lw = -math.exp(-0.5) * _sigmoid(w0_ref[...] + rd[3])
        lw_ref[...] = lw
        hi = lw.astype(BF16)
        lo = (lw - hi.astype(F32)).astype(BF16)
        ri = lax.broadcasted_iota(jnp.int32, (ts, ts), 0)
        ci = lax.broadcasted_iota(jnp.int32, (ts, ts), 1)
        tri = jnp.where((ri // WKV_CHUNK == ci // WKV_CHUNK) & (ri >= ci), 1.0, 0.0).astype(BF16)

        def tail(rs):
            r, k, v = rd[0, rs, :], rd[1, rs, :], rd[2, rs, :]
            a = _sigmoid(a0_ref[...] + rd[4, rs, :])
            g_ref[rs, :] = rd[5, rs, :]
            if vres:
                v = v + (vf_ref[rs, :] - v) * _sigmoid(v0_ref[...] + rd[6, rs, :])
            else:
                vf_out_ref[rs, :] = v
            vb_ref[rs, :] = v.astype(BF16)
            kk = k * kk_ref[...]
            kk = kk * lax.rsqrt(jnp.maximum(_head_sum(kk * kk), 1e-24))
            k = k * (1.0 + (a - 1.0) * ka_ref[...])
            bonus_ref[rs, :] = _head_sum(r * k * rk_ref[...]) * v
            r_ref[rs, :] = r.astype(BF16)
            k_ref[rs, :] = k.astype(BF16)
            nkk_ref[rs, :] = (-kk).astype(BF16)
            kka_ref[rs, :] = (kk * a).astype(BF16)

        quarter = ts // 4
        wr[0] = dot(xr, wr_ref)
        c2 = jnp.dot(tri, jnp.concatenate([hi, lo], axis=1), preferred_element_type=F32)
        d = lw.shape[1]
        cum_ref[...] = c2[:, :d] + c2[:, d:]
        tail(slice(0, quarter))
        wr[1] = dot(xk, wk_ref)
        tail(slice(quarter, 2 * quarter))
        wr[2] = dot(xv, wv_ref)
        tail(slice(2 * quarter, 3 * quarter))
        wr[3] = dot(jnp.tanh(hw).astype(BF16), w2_ref)
        wr[4] = dot(ha.astype(BF16), a2_ref)
        wr[5] = dot(_sigmoid(hg).astype(BF16), g2_ref)
        if vres:
            wr[6] = dot(hv.astype(BF16), v2_ref)
        tail(slice(3 * quarter, ts))

    @pl.when(s % 2 == 0)
    def _():
        step(buf_b, buf_a)

    @pl.when(s % 2 == 1)
    def _():
        step(buf_a, buf_b)


def _rwkv_pre(h, gmix, p, v_first, vres, *, ts):
    bsz, seq, d = h.shape
    nblk = seq // ts
    row = lambda v: v.reshape(1, -1)
    cur = pl.BlockSpec((None, ts, d), lambda b, s: (b, jnp.minimum(s, nblk - 1), 0))
    lag = pl.BlockSpec((None, ts, d), lambda b, s: (b, jnp.maximum(s - 1, 0), 0))
    wb = lambda n: p[n].astype(BF16)
    args = [h, row(gmix), p["mu"], wb("w_r"), wb("w_k"), wb("w_v"), row(p["w0"]), wb("w1"),
            wb("w2"), row(p["a0"]), wb("a1"), wb("a2"), wb("g1"), wb("g2"), row(p["k_k"]),
            row(p["k_a"]), row(p["r_k"])]
    specs = [cur] + [_const_spec(a.shape) for a in args[1:]]
    if vres is not None:
        v0, v1, v2 = vres
        extra = [row(v0), v1.astype(BF16), v2.astype(BF16)]
        args += [v_first] + extra
        specs += [lag] + [_const_spec(a.shape) for a in extra]
    tok_bf = jax.ShapeDtypeStruct(h.shape, BF16)
    tok_f32 = jax.ShapeDtypeStruct(h.shape, F32)
    out_shape = [tok_bf] * 5 + [tok_f32] * (4 if vres is not None else 5)
    nproj = 7 if vres is not None else 6
    outs = pl.pallas_call(
        functools.partial(_rwkv_pre_kernel, ts=ts, vres=vres is not None),
        grid=(bsz, nblk + 1),
        in_specs=specs,
        out_specs=[lag] * len(out_shape),
        out_shape=out_shape,
        scratch_shapes=[pltpu.VMEM((8, d), F32), pltpu.VMEM((nproj, ts, d), F32),
                        pltpu.VMEM((nproj, ts, d), F32)],
        compiler_params=pltpu.CompilerParams(
            dimension_semantics=("arbitrary", "arbitrary"), vmem_limit_bytes=VMEM_LIMIT),
        name="rwkv_pre_vres" if vres is not None else "rwkv_pre",
    )(*args)
    scan_in, gate, bonus = outs[:7], outs[7], outs[8]
    if vres is None:
        v_first = outs[9]
    return scan_in, gate, bonus, v_first


def _wkv_kernel(r_ref, k_ref, nkk_ref, kka_ref, v_ref, lw_ref, cum_ref, y_ref, s_ref,
                *, nbatch, npair, nchunk):
    ch = WKV_CHUNK
    pair = 2 * ch

    @pl.when(pl.program_id(1) == 0)
    def _():
        s_ref[...] = jnp.zeros(s_ref.shape, F32)

    row = lax.broadcasted_iota(jnp.int32, (pair, pair), 0)
    col = lax.broadcasted_iota(jnp.int32, (pair, pair), 1)
    same = (row // ch) == (col // ch)
    same_bf = jnp.where(same, 1.0, 0.0).astype(BF16)
    trow = lax.broadcasted_iota(jnp.int32, (ch, pair), 0)
    tcol = lax.broadcasted_iota(jnp.int32, (ch, pair), 1) % ch
    strict = trow > tcol
    incl = trow >= tcol
    eye = jnp.where(trow == tcol, 1.0, 0.0).astype(F32)
    nt = (((1,), (1,)), ((), ()))
    tn = (((0,), (0,)), ((), ()))
    dot = functools.partial(jnp.dot, preferred_element_type=F32)

    def stack(x):
        return jnp.concatenate([x, x], axis=0) * same_bf

    nchain = nbatch * npair
    items = [(ck, q) for ck in range(nchunk) for q in range(nchain)]
    sl = {(ck, q): (q // npair, slice(ck * ch, (ck + 1) * ch),
                    slice((q % npair) * LANES, (q % npair + 1) * LANES)) for ck, q in items}
    ar_lhs, v_bd, a_k, a_rb, power, tinv, bk_end, g_end = {}, {}, {}, {}, {}, {}, {}, {}
    for it in items:
        bi, rs, ls = sl[it]
        cum = cum_ref[bi, rs, ls]
        g_row = jnp.exp(cum[ch - 1:ch, :])
        g_end[it] = jnp.transpose(jnp.broadcast_to(g_row, (8, LANES)))[:, :1]
        inv = jnp.exp(-cum)
        inv_b = inv.astype(BF16)
        end_b = (inv * g_row).astype(BF16)
        k, kka = k_ref[bi, rs, ls], kka_ref[bi, rs, ls]
        bk_end[it] = jnp.concatenate([kka * end_b, k * end_b], axis=0)
        v_bd[it] = stack(v_ref[bi, rs, ls])
        ar_lhs[it] = jnp.concatenate(
            [nkk_ref[bi, rs, ls] * jnp.exp(cum - lw_ref[bi, rs, ls]).astype(BF16),
             r_ref[bi, rs, ls] * jnp.exp(cum).astype(BF16)], axis=0)
        rhs = jnp.concatenate([stack(kka * inv_b), stack(k * inv_b)], axis=0)
        amat = dot(ar_lhs[it], rhs.T)
        power[it] = jnp.where(strict, amat[:ch, :pair], 0.0)
        tinv[it] = eye + power[it]
        a_rb[it] = jnp.where(incl, amat[ch:, :pair], 0.0).astype(BF16)
        a_k[it] = jnp.concatenate([jnp.where(strict, amat[:ch, pair:], 0.0),
                                   jnp.where(incl, amat[ch:, pair:], 0.0)], axis=0).astype(BF16)

    for it in items:
        pb = power[it].astype(BF16)
        power[it] = dot(pb, stack(pb))
    for _ in range(int(math.log2(ch)) - 2):
        for it in items:
            pb = power[it].astype(BF16)
            both = dot(jnp.concatenate([pb, tinv[it].astype(BF16)], axis=0), stack(pb))
            power[it] = both[:ch]
            tinv[it] = tinv[it] + both[ch:]
    for it in items:
        tinv[it] = (tinv[it] + dot(tinv[it].astype(BF16),
                                   stack(power[it].astype(BF16)))).astype(BF16)
    av = {it: dot(a_k[it], v_bd[it]) for it in items}

    state = [s_ref[q] for q in range(nchain)]
    for ck in range(nchunk):
        ar, u = {}, {}
        for q in range(nchain):
            ar[q] = dot(ar_lhs[ck, q], state[q].astype(BF16)) + av[ck, q]
        for q in range(nchain):
            u[q] = dot(tinv[ck, q], stack(ar[q][:ch].astype(BF16))).astype(BF16)
        for q in range(nchain):
            bi, rs, ls = sl[ck, q]
            y_ref[bi, rs, ls] = ar[q][ch:] + dot(a_rb[ck, q], stack(u[q]))
            upd = lax.dot_general(bk_end[ck, q],
                                  jnp.concatenate([u[q], v_ref[bi, rs, ls]], axis=0),
                                  tn, preferred_element_type=F32)
            state[q] = state[q] * g_end[ck, q] + jnp.where(same, upd, 0.0)
    for q in range(nchain):
        s_ref[q] = state[q]


def _wkv(scan_in, *, npair, nchunk):
    rt = scan_in[0]
    bsz, seq, d = rt.shape
    width = npair * LANES
    rows = nchunk * WKV_CHUNK
    tok = pl.BlockSpec((bsz, rows, width), lambda p, c: (0, c, p))
    return pl.pallas_call(
        functools.partial(_wkv_kernel, nbatch=bsz, npair=npair, nchunk=nchunk),
        grid=(d // width, seq // rows),
        in_specs=[tok] * 7,
        out_specs=tok,
        out_shape=jax.ShapeDtypeStruct((bsz, seq, d), F32),
        scratch_shapes=[pltpu.VMEM((bsz * npair, LANES, LANES), F32)],
        compiler_params=pltpu.CompilerParams(
            dimension_semantics=("arbitrary", "arbitrary"), vmem_limit_bytes=VMEM_LIMIT),
        name="wkv",
    )(*scan_in)


def kernel(x, mix_norm, ffn_norm, final_norm, ab_w_in, ab_w_out, sgu_gain, sgu_w_s, sgu_bias, pool_w, pool_scale, rwkv_mu, rwkv_w_r, rwkv_w_k, rwkv_w_v, rwkv_w_o, rwkv_w0, rwkv_w1, rwkv_w2, rwkv_a0, rwkv_a1, rwkv_a2, rwkv_g1, rwkv_g2, rwkv_k_k, rwkv_k_a, rwkv_r_k, rwkv_ln_w, rwkv_ln_b, rwkv_v0, rwkv_v1, rwkv_v2, ffn_w_gate, ffn_w_up, ffn_w_down):
    depth = mix_norm.shape[0]
    seq = x.shape[1]
    ts_ab = min(512, seq)
    ts_pre = min(256, seq)
    tm = min(512, seq)
    h = x
    v_first = None
    for layer in range(depth):
        i = layer // 2
        post = None
        if layer % 2 == 0:
            h = _ab_mixer(h, mix_norm[layer], ab_w_in[i], ab_w_out[i], sgu_gain[i], sgu_w_s[i],
                          sgu_bias[i], pool_w[i], pool_scale[i], ts=ts_ab)
        else:
            p = dict(mu=rwkv_mu[i], w_r=rwkv_w_r[i], w_k=rwkv_w_k[i], w_v=rwkv_w_v[i],
                     w0=rwkv_w0[i], w1=rwkv_w1[i], w2=rwkv_w2[i], a0=rwkv_a0[i], a1=rwkv_a1[i],
                     a2=rwkv_a2[i], g1=rwkv_g1[i], g2=rwkv_g2[i], k_k=rwkv_k_k[i],
                     k_a=rwkv_k_a[i], r_k=rwkv_r_k[i])
            vres = None if i == 0 else (rwkv_v0[i - 1], rwkv_v1[i - 1], rwkv_v2[i - 1])
            scan_in, gate, bonus, v_first = _rwkv_pre(h, mix_norm[layer], p, v_first, vres,
                                                       ts=ts_pre)
            y = _wkv(scan_in, npair=8, nchunk=2)
            post = (y, gate, bonus, rwkv_ln_w[i], rwkv_ln_b[i], rwkv_w_o[i])
        h = _ffn(h, ffn_norm[layer], ffn_w_gate[layer], ffn_w_up[layer], ffn_w_down[layer],
                 tm=tm, post=post, final_norm=final_norm if layer == depth - 1 else None)
    return h
```

```python
import functools
import math

import jax
import jax.numpy as jnp
from jax import lax
from jax.experimental import pallas as pl
from jax.experimental.pallas import tpu as pltpu

F32 = jnp.float32
BF16 = jnp.bfloat16

EPS = 1e-6
GN_EPS = 64e-5
LANES = 128
HEAD_DIM = 64
WKV_CHUNK = 64
GMLP_BLOCK = 128
GMLP_CAUSAL_CHUNK = 64
POOL_WINDOWS = (2, 4, 8, 16)
POOL_HALO = 16
VMEM_LIMIT = 56 * 1024 * 1024


def _const_spec(shape):
    nd = len(shape)
    return pl.BlockSpec(shape, lambda *_: (0,) * nd, pipeline_mode=pl.Buffered(1))


def _rms(x, g):
    return x * lax.rsqrt(jnp.mean(x * x, axis=-1, keepdims=True) + EPS) * g


def _bdot(a, b):
    return jnp.dot(a.astype(BF16), b.astype(BF16), preferred_element_type=F32)


def _head_sum(x):
    lane_lo = lax.broadcasted_iota(jnp.int32, (1, LANES), 1) < HEAD_DIM
    outs = []
    for j in range(x.shape[1] // LANES):
        blk = x[:, j * LANES:(j + 1) * LANES]
        lo = jnp.sum(jnp.where(lane_lo, blk, 0.0), axis=-1, keepdims=True)
        tot = jnp.sum(blk, axis=-1, keepdims=True)
        outs.append(jnp.where(lane_lo, lo, tot - lo))
    return jnp.concatenate(outs, axis=1)


def _ab_body(x_ref, gmix_ref, win_ref, wout_ref, sgain_ref, ws_ref, sbias_ref, pw_ref, pscale_ref,
             dst_ref, zs_ref, *, s, ts, a_width, between=()):
    between = list(between) + [None] * 3
    half = ts // 2
    dot = functools.partial(jnp.dot, preferred_element_type=F32)
    gelu = lambda z: 0.5 * z * (1.0 + lax.erf(z * (1.0 / math.sqrt(2.0))))

    ri = lax.broadcasted_iota(jnp.int32, (GMLP_BLOCK, GMLP_BLOCK), 0) // GMLP_CAUSAL_CHUNK
    ci = lax.broadcasted_iota(jnp.int32, (GMLP_BLOCK, GMLP_BLOCK), 1) // GMLP_CAUSAL_CHUNK
    keep = ri >= ci
    w_s = [jnp.where(keep, ws_ref[hd], 0.0).astype(BF16) for hd in range(a_width // LANES)]

    def in_proj(r0):
        h = x_ref[r0:r0 + half, :]
        hn = _rms(h, gmix_ref[...]).astype(BF16)
        return (h, dot(hn, win_ref[:, :a_width]), dot(hn, win_ref[:, a_width:2 * a_width]),
                dot(hn, win_ref[:, 2 * a_width:]))

    def gate_inputs(zu, zv):
        return gelu(zu), _rms(gelu(zv), sgain_ref[...]).astype(BF16)

    def spatial_gate(u, vn):
        heads = []
        for hd, w in enumerate(w_s):
            cs = slice(hd * LANES, (hd + 1) * LANES)
            bias = sbias_ref[:, cs]
            rows = [dot(w, vn[b0:b0 + GMLP_BLOCK, cs]) + bias
                    for b0 in range(0, half, GMLP_BLOCK)]
            heads.append(jnp.concatenate(rows, axis=0))
        return u * jnp.concatenate(heads, axis=1)

    def pool(r0, zb):
        base = POOL_HALO + r0
        zs_ref[base:base + half, :] = zb
        t1 = (lax.broadcasted_iota(jnp.int32, (half, 1), 0) + (s * ts + r0 + 1)).astype(F32)
        groups = []
        for gi, win in enumerate(POOL_WINDOWS):
            cs = slice(gi * LANES, (gi + 1) * LANES)
            cur = zb[:, cs]
            acc = cur
            for j in range(1, win):
                acc = acc + zs_ref[pl.ds(base - j, half), cs]
            pooled = acc / jnp.minimum(t1, float(win)) - cur
            groups.append(_bdot(pooled, pw_ref[gi]))
        return jnp.concatenate(groups, axis=1) * pscale_ref[...]

    def out_proj(r0, h, y_a, y_b):
        cat = jnp.concatenate([y_a, y_b], axis=1)
        dst_ref[r0:r0 + half, :] = h + _bdot(cat, wout_ref[...])

    h0, zu0, zv0, zp0 = in_proj(0)
    h1, zu1, zv1, zp1 = in_proj(half)
    if between[0]:
        between[0]()
    u0, vn0 = gate_inputs(zu0, zv0)
    ya0 = spatial_gate(u0, vn0)
    yb0 = pool(0, zp0)
    u1, vn1 = gate_inputs(zu1, zv1)
    out_proj(0, h0, ya0, yb0)
    if between[1]:
        between[1]()
    ya1 = spatial_gate(u1, vn1)
    yb1 = pool(half, zp1)
    out_proj(half, h1, ya1, yb1)
    zs_ref[0:POOL_HALO, :] = zs_ref[ts:ts + POOL_HALO, :]
    if between[2]:
        between[2]()


def _ab_ffn_kernel(x_ref, gmix_ref, win_ref, wout_ref, sgain_ref, ws_ref, sbias_ref, pw_ref,
                   pscale_ref, gn_ref, wg_ref, wu_ref, wd_ref, o_ref, zs_ref, buf_a, buf_b,
                   *, ts, a_width):
    s = pl.program_id(1)

    @pl.when(s == 0)
    def _():
        zs_ref[0:POOL_HALO, :] = jnp.zeros((POOL_HALO, zs_ref.shape[1]), F32)
        buf_b[...] = jnp.zeros(buf_b.shape, F32)

    def step(rd, wr):
        ffn = {}

        def gate():
            ffn["h"] = rd[...]
            ffn["hn"] = _rms(ffn["h"], gn_ref[...]).astype(BF16)
            ffn["gate"] = jnp.dot(ffn["hn"], wg_ref[...], preferred_element_type=F32)

        def up():
            ffn["up"] = jnp.dot(ffn["hn"], wu_ref[...], preferred_element_type=F32)

        def down():
            mid = ffn["gate"] * jax.nn.sigmoid(ffn["gate"]) * ffn["up"]
            o_ref[...] = ffn["h"] + _bdot(mid, wd_ref[...])

        _ab_body(x_ref, gmix_ref, win_ref, wout_ref, sgain_ref, ws_ref, sbias_ref, pw_ref,
                 pscale_ref, wr, zs_ref, s=s, ts=ts, a_width=a_width, between=(gate, up, down))

    @pl.when(s % 2 == 0)
    def _():
        step(buf_b, buf_a)

    @pl.when(s % 2 == 1)
    def _():
        step(buf_a, buf_b)


def _ab_ffn(h, gmix, w_in, w_out, sgain, w_s, s_bias, pool_w, pool_scale, gnorm, w_gate, w_up,
            w_down, *, ts):
    bsz, seq, d = h.shape
    nblk = seq // ts
    a_width = sgain.shape[0]
    b_width = pool_scale.shape[0]
    n_heads = w_s.shape[0]
    bias_full = jnp.repeat(s_bias.T, a_width // n_heads, axis=1)
    row = lambda v: v.reshape(1, -1)
    cur = pl.BlockSpec((None, ts, d), lambda b, s: (b, jnp.minimum(s, nblk - 1), 0))
    lag = pl.BlockSpec((None, ts, d), lambda b, s: (b, jnp.maximum(s - 1, 0), 0))
    params = [row(gmix), w_in.astype(BF16), w_out.astype(BF16), row(sgain), w_s, bias_full,
              pool_w.astype(BF16), row(pool_scale), row(gnorm), w_gate.astype(BF16),
              w_up.astype(BF16), w_down.astype(BF16)]
    return pl.pallas_call(
        functools.partial(_ab_ffn_kernel, ts=ts, a_width=a_width),
        grid=(bsz, nblk + 1),
        in_specs=[cur] + [_const_spec(a.shape) for a in params],
        out_specs=lag,
        out_shape=jax.ShapeDtypeStruct(h.shape, F32),
        scratch_shapes=[pltpu.VMEM((ts + POOL_HALO, b_width), F32), pltpu.VMEM((ts, d), F32),
                        pltpu.VMEM((ts, d), F32)],
        compiler_params=pltpu.CompilerParams(
            dimension_semantics=("arbitrary", "arbitrary"), vmem_limit_bytes=VMEM_LIMIT),
        name="ab_ffn",
    )(h, *params)


def _swiglu(h, gn_ref, wg_ref, wu_ref, wd_ref, fin_ref, between=()):
    between = list(between) + [None] * 3
    hn = _rms(h, gn_ref[...]).astype(BF16)
    gate = jnp.dot(hn, wg_ref[...], preferred_element_type=F32)
    if between[0]:
        between[0]()
    up = jnp.dot(hn, wu_ref[...], preferred_element_type=F32)
    if between[1]:
        between[1]()
    mid = gate * jax.nn.sigmoid(gate) * up
    out = h + _bdot(mid, wd_ref[...])
    if between[2]:
        between[2]()
    if fin_ref is not None:
        out = _rms(out, fin_ref[...])
    return out


def _ffn_post_kernel(x_ref, y_ref, g_ref, bonus_ref, lnw_ref, lnb_ref, wo_ref, gn_ref, wg_ref,
                     wu_ref, wd_ref, *rest, tm):
    *fin, o_ref, buf_a, buf_b = rest
    s = pl.program_id(0)

    @pl.when(s == 0)
    def _():
        buf_b[...] = jnp.zeros(buf_b.shape, BF16)

    def prepare(wr, rs):
        y = y_ref[rs, :]
        mean = _head_sum(y) * (1.0 / HEAD_DIM)
        dev = y - mean
        var = _head_sum(dev * dev) * (1.0 / HEAD_DIM)
        yn = dev * lax.rsqrt(var + GN_EPS) * lnw_ref[...] + lnb_ref[...]
        wr[rs, :] = ((yn + bonus_ref[rs, :]) * g_ref[rs, :]).astype(BF16)

    def step(rd, wr):
        quarter = tm // 4
        pieces = [functools.partial(prepare, wr, slice(q * quarter, (q + 1) * quarter))
                  for q in range(4)]
        h = x_ref[...] + jnp.dot(rd[...], wo_ref[...], preferred_element_type=F32)
        pieces[0]()
        o_ref[...] = _swiglu(h, gn_ref, wg_ref, wu_ref, wd_ref, fin[0] if fin else None,
                             between=pieces[1:])

    @pl.when(s % 2 == 0)
    def _():
        step(buf_b, buf_a)

    @pl.when(s % 2 == 1)
    def _():
        step(buf_a, buf_b)


def _ffn_post(h, gnorm, w_gate, w_up, w_down, post, *, tm, final_norm=None):
    bsz, seq, d = h.shape
    n = bsz * seq
    nblk = n // tm
    row = lambda v: v.reshape(1, -1)
    y, g, bonus, ln_w, ln_b, w_o = post
    params = [row(ln_w), row(ln_b), w_o.astype(BF16), row(gnorm), w_gate.astype(BF16),
              w_up.astype(BF16), w_down.astype(BF16)]
    if final_norm is not None:
        params.append(row(final_norm))
    cur = pl.BlockSpec((tm, d), lambda i: (jnp.minimum(i, nblk - 1), 0))
    lag = pl.BlockSpec((tm, d), lambda i: (jnp.maximum(i - 1, 0), 0))
    out = pl.pallas_call(
        functools.partial(_ffn_post_kernel, tm=tm), grid=(nblk + 1,),
        in_specs=[lag, cur, cur, cur] + [_const_spec(a.shape) for a in params],
        out_specs=lag, out_shape=jax.ShapeDtypeStruct((n, d), F32),
        scratch_shapes=[pltpu.VMEM((tm, d), BF16), pltpu.VMEM((tm, d), BF16)],
        compiler_params=pltpu.CompilerParams(
            dimension_semantics=("arbitrary",), vmem_limit_bytes=VMEM_LIMIT),
        name="ffn_post",
    )(h.reshape(n, d), y.reshape(n, d), g.reshape(n, d), bonus.reshape(n, d), *params)
    return out.reshape(bsz, seq, d)


def _sigmoid(x):
    return 0.5 * jnp.tanh(0.5 * x) + 0.5


def _rwkv_pre_kernel(*refs, ts, vres):
    it = iter(refs)
    (x_ref, gmix_ref, mu_ref, wr_ref, wk_ref, wv_ref, w0_ref, w1_ref, w2_ref, a0_ref, a1_ref,
     a2_ref, g1_ref, g2_ref, kk_ref, ka_ref, rk_ref) = (next(it) for _ in range(17))
    if vres:
        vf_ref, v0_ref, v1_ref, v2_ref = (next(it) for _ in range(4))
    (r_ref, k_ref, nkk_ref, kka_ref, vb_ref, lw_ref, cum_ref, g_ref,
     bonus_ref) = (next(it) for _ in range(9))
    if not vres:
        vf_out_ref = next(it)
    prev_ref, buf_a, buf_b = next(it), next(it), next(it)

    s = pl.program_id(1)

    @pl.when(s == 0)
    def _():
        prev_ref[...] = jnp.zeros(prev_ref.shape, F32)
        buf_b[...] = jnp.zeros(buf_b.shape, F32)

    dot = lambda a, b_ref: jnp.dot(a, b_ref[...], preferred_element_type=F32)

    def step(rd, wr):
        hn = _rms(x_ref[...], gmix_ref[...])
        first = lax.broadcasted_iota(jnp.int32, (ts, 1), 0) == 0
        hprev = jnp.where(first, prev_ref[0:1, :], pltpu.roll(hn, 1, axis=0))
        prev_ref[0:1, :] = hn[ts - 1:ts, :]
        hn_b = hn.astype(BF16)
        xx_b = (hprev - hn).astype(BF16)
        mu_b = mu_ref[...].astype(BF16)
        xr, xw, xk, xv, xa, xg = (hn_b + xx_b * mu_b[i:i + 1, :] for i in range(6))
        hw, ha, hg = dot(xw, w1_ref), dot(xa, a1_ref), dot(xg, g1_ref)
        if vres:
            hv = dot(xv, v1_ref)

        lw = -math.exp(-0.5) * _sigmoid(w0_ref[...] + rd[3])
        lw_ref[...] = lw
        hi = lw.astype(BF16)
        lo = (lw - hi.astype(F32)).astype(BF16)
        ri = lax.broadcasted_iota(jnp.int32, (ts, ts), 0)
        ci = lax.broadcasted_iota(jnp.int32, (ts, ts), 1)
        tri = jnp.where((ri // WKV_CHUNK == ci // WKV_CHUNK) & (ri >= ci), 1.0, 0.0).astype(BF16)

        def tail(rs):
            r, k, v = rd[0, rs, :], rd[1, rs, :], rd[2, rs, :]
            a = _sigmoid(a0_ref[...] + rd[4, rs, :])
            g_ref[rs, :] = rd[5, rs, :]
            if vres:
                v = v + (vf_ref[rs, :] - v) * _sigmoid(v0_ref[...] + rd[6, rs, :])
            else:
                vf_out_ref[rs, :] = v
            vb_ref[rs, :] = v.astype(BF16)
            kk = k * kk_ref[...]
            kk = kk * lax.rsqrt(jnp.maximum(_head_sum(kk * kk), 1e-24))
            ka = ka_ref[...]
            k = k * ((1.0 - ka) + a * ka)
            bonus_ref[rs, :] = _head_sum(r * k * rk_ref[...]) * v
            r_ref[rs, :] = r.astype(BF16)
            k_ref[rs, :] = k.astype(BF16)
            nkk_ref[rs, :] = (-kk).astype(BF16)
            kka_ref[rs, :] = (kk * a).astype(BF16)

        quarter = ts // 4
        wr[0] = dot(xr, wr_ref)
        c2 = jnp.dot(tri, jnp.concatenate([hi, lo], axis=1), preferred_element_type=F32)
        d = lw.shape[1]
        cum_ref[...] = c2[:, :d] + c2[:, d:]
        tail(slice(0, quarter))
        wr[1] = dot(xk, wk_ref)
        tail(slice(quarter, 2 * quarter))
        wr[2] = dot(xv, wv_ref)
        tail(slice(2 * quarter, 3 * quarter))
        wr[3] = dot(jnp.tanh(hw).astype(BF16), w2_ref)
        wr[4] = dot(ha.astype(BF16), a2_ref)
        wr[5] = dot(_sigmoid(hg).astype(BF16), g2_ref)
        if vres:
            wr[6] = dot(hv.astype(BF16), v2_ref)
        tail(slice(3 * quarter, ts))

    @pl.when(s % 2 == 0)
    def _():
        step(buf_b, buf_a)

    @pl.when(s % 2 == 1)
    def _():
        step(buf_a, buf_b)


def _rwkv_pre(h, gmix, p, v_first, vres, *, ts):
    bsz, seq, d = h.shape
    nblk = seq // ts
    row = lambda v: v.reshape(1, -1)
    cur = pl.BlockSpec((None, ts, d), lambda b, s: (b, jnp.minimum(s, nblk - 1), 0))
    lag = pl.BlockSpec((None, ts, d), lambda b, s: (b, jnp.maximum(s - 1, 0), 0))
    wb = lambda n: p[n].astype(BF16)
    args = [h, row(gmix), p["mu"], wb("w_r"), wb("w_k"), wb("w_v"), row(p["w0"]), wb("w1"),
            wb("w2"), row(p["a0"]), wb("a1"), wb("a2"), wb("g1"), wb("g2"), row(p["k_k"]),
            row(p["k_a"]), row(p["r_k"])]
    specs = [cur] + [_const_spec(a.shape) for a in args[1:]]
    if vres is not None:
        v0, v1, v2 = vres
        extra = [row(v0), v1.astype(BF16), v2.astype(BF16)]
        args += [v_first] + extra
        specs += [lag] + [_const_spec(a.shape) for a in extra]
    tok_bf = jax.ShapeDtypeStruct(h.shape, BF16)
    tok_f32 = jax.ShapeDtypeStruct(h.shape, F32)
    out_shape = [tok_bf] * 5 + [tok_f32] * (4 if vres is not None else 5)
    nproj = 7 if vres is not None else 6
    outs = pl.pallas_call(
        functools.partial(_rwkv_pre_kernel, ts=ts, vres=vres is not None),
        grid=(bsz, nblk + 1),
        in_specs=specs,
        out_specs=[lag] * len(out_shape),
        out_shape=out_shape,
        scratch_shapes=[pltpu.VMEM((8, d), F32), pltpu.VMEM((nproj, ts, d), F32),
                        pltpu.VMEM((nproj, ts, d), F32)],
        compiler_params=pltpu.CompilerParams(
            dimension_semantics=("arbitrary", "arbitrary"), vmem_limit_bytes=VMEM_LIMIT),
        name="rwkv_pre_vres" if vres is not None else "rwkv_pre",
    )(*args)
    scan_in, gate, bonus = outs[:7], outs[7], outs[8]
    if vres is None:
        v_first = outs[9]
    return scan_in, gate, bonus, v_first


def _wkv_kernel(r_ref, k_ref, nkk_ref, kka_ref, v_ref, lw_ref, cum_ref, y_ref, s_ref,
                *, nbatch, npair, nchunk):
    ch = WKV_CHUNK
    pair = 2 * ch

    @pl.when(pl.program_id(1) == 0)
    def _():
        s_ref[...] = jnp.zeros(s_ref.shape, F32)

    row = lax.broadcasted_iota(jnp.int32, (pair, pair), 0)
    col = lax.broadcasted_iota(jnp.int32, (pair, pair), 1)
    same = (row // ch) == (col // ch)
    same_bf = jnp.where(same, 1.0, 0.0).astype(BF16)
    trow = lax.broadcasted_iota(jnp.int32, (ch, pair), 0)
    tcol = lax.broadcasted_iota(jnp.int32, (ch, pair), 1) % ch
    strict = trow > tcol
    incl = trow >= tcol
    eye = jnp.where(trow == tcol, 1.0, 0.0).astype(F32)
    nt = (((1,), (1,)), ((), ()))
    tn = (((0,), (0,)), ((), ()))
    dot = functools.partial(jnp.dot, preferred_element_type=F32)

    def stack(x):
        return jnp.concatenate([x, x], axis=0) * same_bf

    nchain = nbatch * npair
    items = [(ck, q) for ck in range(nchunk) for q in range(nchain)]
    sl = {(ck, q): (q // npair, slice(ck * ch, (ck + 1) * ch),
                    slice((q % npair) * LANES, (q % npair + 1) * LANES)) for ck, q in items}
    ar_lhs, v_bd, a_k, a_rb, power, tinv, bk_end, g_end = {}, {}, {}, {}, {}, {}, {}, {}
    av = {}
    for ck in range(nchunk):
        group = [(ck, q) for q in range(nchain)]
        for it in group:
            bi, rs, ls = sl[it]
            cum = cum_ref[bi, rs, ls]
            g_row = jnp.exp(cum[ch - 1:ch, :])
            g_end[it] = jnp.transpose(jnp.broadcast_to(g_row, (8, LANES)))[:, :1]
            inv = jnp.exp(-cum)
            inv_b = inv.astype(BF16)
            end_b = (inv * g_row).astype(BF16)
            k, kka = k_ref[bi, rs, ls], kka_ref[bi, rs, ls]
            bk_end[it] = jnp.concatenate([kka * end_b, k * end_b], axis=0)
            v_bd[it] = stack(v_ref[bi, rs, ls])
            ar_lhs[it] = jnp.concatenate(
                [nkk_ref[bi, rs, ls] * jnp.exp(cum - lw_ref[bi, rs, ls]).astype(BF16),
                 r_ref[bi, rs, ls] * jnp.exp(cum).astype(BF16)], axis=0)
            rhs = jnp.concatenate([stack(kka * inv_b), stack(k * inv_b)], axis=0)
            amat = dot(ar_lhs[it], rhs.T)
            power[it] = jnp.where(strict, amat[:ch, :pair], 0.0)
            tinv[it] = eye + power[it]
            a_rb[it] = jnp.where(incl, amat[ch:, :pair], 0.0).astype(BF16)
            a_k[it] = jnp.concatenate([jnp.where(strict, amat[:ch, pair:], 0.0),
                                       jnp.where(incl, amat[ch:, pair:], 0.0)],
                                      axis=0).astype(BF16)

        for it in group:
            pb = power[it].astype(BF16)
            power[it] = dot(pb, stack(pb))
        for _ in range(int(math.log2(ch)) - 2):
            for it in group:
                pb = power[it].astype(BF16)
                both = dot(jnp.concatenate([pb, tinv[it].astype(BF16)], axis=0), stack(pb))
                power[it] = both[:ch]
                tinv[it] = tinv[it] + both[ch:]
        for it in group:
            tinv[it] = (tinv[it] + dot(tinv[it].astype(BF16),
                                       stack(power[it].astype(BF16)))).astype(BF16)
        for it in group:
            av[it] = dot(a_k[it], v_bd[it])

    state = [s_ref[q] for q in range(nchain)]
    for ck in range(nchunk):
        ar, u = {}, {}
        for q in range(nchain):
            ar[q] = dot(ar_lhs[ck, q], state[q].astype(BF16)) + av[ck, q]
        for q in range(nchain):
            u[q] = dot(tinv[ck, q], stack(ar[q][:ch].astype(BF16))).astype(BF16)
        for q in range(nchain):
            bi, rs, ls = sl[ck, q]
            y_ref[bi, rs, ls] = ar[q][ch:] + dot(a_rb[ck, q], stack(u[q]))
            upd = lax.dot_general(bk_end[ck, q],
                                  jnp.concatenate([u[q], v_ref[bi, rs, ls]], axis=0),
                                  tn, preferred_element_type=F32)
            state[q] = state[q] * g_end[ck, q] + jnp.where(same, upd, 0.0)
    for q in range(nchain):
        s_ref[q] = state[q]


def _wkv(scan_in, *, npair, nchunk):
    bsz, seq, d = scan_in[0].shape
    width = npair * LANES
    rows = nchunk * WKV_CHUNK
    tok = pl.BlockSpec((bsz, rows, width), lambda p, c: (0, c, p))
    return pl.pallas_call(
        functools.partial(_wkv_kernel, nbatch=bsz, npair=npair, nchunk=nchunk),
        grid=(d // width, seq // rows),
        in_specs=[tok] * len(scan_in),
        out_specs=tok,
        out_shape=jax.ShapeDtypeStruct((bsz, seq, d), F32),
        scratch_shapes=[pltpu.VMEM((bsz * npair, LANES, LANES), F32)],
        compiler_params=pltpu.CompilerParams(
            dimension_semantics=("arbitrary", "arbitrary"), vmem_limit_bytes=VMEM_LIMIT),
        name="wkv",
    )(*scan_in)


def kernel(x, mix_norm, ffn_norm, final_norm, ab_w_in, ab_w_out, sgu_gain, sgu_w_s, sgu_bias, pool_w, pool_scale, rwkv_mu, rwkv_w_r, rwkv_w_k, rwkv_w_v, rwkv_w_o, rwkv_w0, rwkv_w1, rwkv_w2, rwkv_a0, rwkv_a1, rwkv_a2, rwkv_g1, rwkv_g2, rwkv_k_k, rwkv_k_a, rwkv_r_k, rwkv_ln_w, rwkv_ln_b, rwkv_v0, rwkv_v1, rwkv_v2, ffn_w_gate, ffn_w_up, ffn_w_down):
    depth = mix_norm.shape[0]
    assert depth % 2 == 0, "the final norm is fused into the last (RWKV) layer's FFN"
    seq = x.shape[1]
    ts_ab = min(512, seq)
    ts_pre = min(256, seq)
    tm = min(512, seq)
    h = x
    v_first = None
    for layer in range(depth):
        i = layer // 2
        if layer % 2 == 0:
            h = _ab_ffn(h, mix_norm[layer], ab_w_in[i], ab_w_out[i], sgu_gain[i], sgu_w_s[i],
                        sgu_bias[i], pool_w[i], pool_scale[i], ffn_norm[layer], ffn_w_gate[layer],
                        ffn_w_up[layer], ffn_w_down[layer], ts=ts_ab)
        else:
            p = dict(mu=rwkv_mu[i], w_r=rwkv_w_r[i], w_k=rwkv_w_k[i], w_v=rwkv_w_v[i],
                     w0=rwkv_w0[i], w1=rwkv_w1[i], w2=rwkv_w2[i], a0=rwkv_a0[i], a1=rwkv_a1[i],
                     a2=rwkv_a2[i], g1=rwkv_g1[i], g2=rwkv_g2[i], k_k=rwkv_k_k[i],
                     k_a=rwkv_k_a[i], r_k=rwkv_r_k[i])
            vres = None if i == 0 else (rwkv_v0[i - 1], rwkv_v1[i - 1], rwkv_v2[i - 1])
            scan_in, gate, bonus, v_first = _rwkv_pre(h, mix_norm[layer], p, v_first, vres,
                                                       ts=ts_pre)
            y = _wkv(scan_in, npair=8, nchunk=4)
            post = (y, gate, bonus, rwkv_ln_w[i], rwkv_ln_b[i], rwkv_w_o[i])
            h = _ffn_post(h, ffn_norm[layer], ffn_w_gate[layer], ffn_w_up[layer],
                          ffn_w_down[layer], post, tm=tm,
                          final_norm=final_norm if layer == depth - 1 else None)
    return h
```

```python
import functools
import math

import jax
import jax.numpy as jnp
from jax import lax
from jax.experimental import pallas as pl
from jax.experimental.pallas import tpu as pltpu

F32 = jnp.float32
BF16 = jnp.bfloat16

EPS = 1e-6
GN_EPS = 64e-5
LANES = 128
HEAD_DIM = 64
WKV_CHUNK = 64
GMLP_BLOCK = 128
GMLP_CAUSAL_CHUNK = 64
POOL_WINDOWS = (2, 4, 8, 16)
POOL_HALO = 16
VMEM_LIMIT = 56 * 1024 * 1024


def _const_spec(shape):
    nd = len(shape)
    return pl.BlockSpec(shape, lambda *_: (0,) * nd, pipeline_mode=pl.Buffered(1))


class _Stacked:
    def __init__(self, stack, index):
        self.stack, self.index = stack, index


def _resident(params):
    arrays, specs = [], []
    for a in params:
        if isinstance(a, _Stacked):
            nd, i = a.stack.ndim - 1, a.index
            arrays.append(a.stack)
            specs.append(pl.BlockSpec((None,) + a.stack.shape[1:],
                                      lambda *_, i=i, nd=nd: (i,) + (0,) * nd,
                                      pipeline_mode=pl.Buffered(1)))
        else:
            arrays.append(a)
            specs.append(_const_spec(a.shape))
    return arrays, specs


def _rms(x, g):
    return x * lax.rsqrt(jnp.mean(x * x, axis=-1, keepdims=True) + EPS) * g


def _bdot(a, b):
    return jnp.dot(a.astype(BF16), b.astype(BF16), preferred_element_type=F32)


def _head_sum(x):
    lane_lo = lax.broadcasted_iota(jnp.int32, (1, LANES), 1) < HEAD_DIM
    outs = []
    for j in range(x.shape[1] // LANES):
        blk = x[:, j * LANES:(j + 1) * LANES]
        lo = jnp.sum(jnp.where(lane_lo, blk, 0.0), axis=-1, keepdims=True)
        tot = jnp.sum(blk, axis=-1, keepdims=True)
        outs.append(jnp.where(lane_lo, lo, tot - lo))
    return jnp.concatenate(outs, axis=1)


def _ab_body(x_ref, gmix_ref, win_ref, wout_ref, sgain_ref, ws_ref, sbias_ref, pw_ref, pscale_ref,
             dst_ref, zs_ref, *, s, ts, a_width, between=()):
    between = list(between) + [None] * 3
    half = ts // 2
    dot = functools.partial(jnp.dot, preferred_element_type=F32)
    gelu = lambda z: 0.5 * z * (1.0 + lax.erf(z * (1.0 / math.sqrt(2.0))))

    ri = lax.broadcasted_iota(jnp.int32, (GMLP_BLOCK, GMLP_BLOCK), 0) // GMLP_CAUSAL_CHUNK
    ci = lax.broadcasted_iota(jnp.int32, (GMLP_BLOCK, GMLP_BLOCK), 1) // GMLP_CAUSAL_CHUNK
    keep = ri >= ci
    w_s = [jnp.where(keep, ws_ref[hd], 0.0).astype(BF16) for hd in range(a_width // LANES)]

    def in_proj(r0):
        h = x_ref[r0:r0 + half, :]
        hn = _rms(h, gmix_ref[...]).astype(BF16)
        return (h, dot(hn, win_ref[:, :a_width]), dot(hn, win_ref[:, a_width:2 * a_width]),
                dot(hn, win_ref[:, 2 * a_width:]))

    def gate_inputs(zu, zv):
        return gelu(zu), _rms(gelu(zv), sgain_ref[...]).astype(BF16)

    def spatial_gate(u, vn):
        heads = []
        for hd, w in enumerate(w_s):
            cs = slice(hd * LANES, (hd + 1) * LANES)
            bias = sbias_ref[:, cs]
            rows = [dot(w, vn[b0:b0 + GMLP_BLOCK, cs]) + bias
                    for b0 in range(0, half, GMLP_BLOCK)]
            heads.append(jnp.concatenate(rows, axis=0))
        return u * jnp.concatenate(heads, axis=1)

    def pool(r0, zb):
        base = POOL_HALO + r0
        zs_ref[base:base + half, :] = zb
        t1 = (lax.broadcasted_iota(jnp.int32, (half, 1), 0) + (s * ts + r0 + 1)).astype(F32)
        groups = []
        for gi, win in enumerate(POOL_WINDOWS):
            cs = slice(gi * LANES, (gi + 1) * LANES)
            cur = zb[:, cs]
            acc = cur
            for j in range(1, win):
                acc = acc + zs_ref[pl.ds(base - j, half), cs]
            pooled = acc / jnp.minimum(t1, float(win)) - cur
            groups.append(_bdot(pooled, pw_ref[gi]))
        return jnp.concatenate(groups, axis=1) * pscale_ref[...]

    def out_proj(r0, h, y_a, y_b):
        cat = jnp.concatenate([y_a, y_b], axis=1)
        dst_ref[r0:r0 + half, :] = h + _bdot(cat, wout_ref[...])

    h0, zu0, zv0, zp0 = in_proj(0)
    h1, zu1, zv1, zp1 = in_proj(half)
    if between[0]:
        between[0]()
    u0, vn0 = gate_inputs(zu0, zv0)
    ya0 = spatial_gate(u0, vn0)
    yb0 = pool(0, zp0)
    u1, vn1 = gate_inputs(zu1, zv1)
    out_proj(0, h0, ya0, yb0)
    if between[1]:
        between[1]()
    ya1 = spatial_gate(u1, vn1)
    yb1 = pool(half, zp1)
    out_proj(half, h1, ya1, yb1)
    zs_ref[0:POOL_HALO, :] = zs_ref[ts:ts + POOL_HALO, :]
    if between[2]:
        between[2]()


def _ab_ffn_kernel(x_ref, gmix_ref, win_ref, wout_ref, sgain_ref, ws_ref, sbias_ref, pw_ref,
                   pscale_ref, gn_ref, wg_ref, wu_ref, wd_ref, o_ref, zs_ref, buf_a, buf_b,
                   *, ts, a_width):
    s = pl.program_id(1)

    @pl.when(s == 0)
    def _():
        zs_ref[0:POOL_HALO, :] = jnp.zeros((POOL_HALO, zs_ref.shape[1]), F32)
        buf_b[...] = jnp.zeros(buf_b.shape, F32)

    def step(rd, wr):
        ffn = {}

        def gate():
            ffn["h"] = rd[...]
            ffn["hn"] = _rms(ffn["h"], gn_ref[...]).astype(BF16)
            ffn["gate"] = jnp.dot(ffn["hn"], wg_ref[...], preferred_element_type=F32)

        def up():
            ffn["up"] = jnp.dot(ffn["hn"], wu_ref[...], preferred_element_type=F32)

        def down():
            mid = ffn["gate"] * jax.nn.sigmoid(ffn["gate"]) * ffn["up"]
            o_ref[...] = ffn["h"] + _bdot(mid, wd_ref[...])

        _ab_body(x_ref, gmix_ref, win_ref, wout_ref, sgain_ref, ws_ref, sbias_ref, pw_ref,
                 pscale_ref, wr, zs_ref, s=s, ts=ts, a_width=a_width, between=(gate, up, down))

    @pl.when(s % 2 == 0)
    def _():
        step(buf_b, buf_a)

    @pl.when(s % 2 == 1)
    def _():
        step(buf_a, buf_b)


def _ab_ffn(h, gmix, w_in, w_out, sgain, w_s, s_bias, pool_w, pool_scale, gnorm, w_gate, w_up,
            w_down, *, ts):
    bsz, seq, d = h.shape
    nblk = seq // ts
    a_width = sgain.shape[0]
    b_width = pool_scale.shape[0]
    n_heads = s_bias.shape[0]
    bias_full = jnp.repeat(s_bias.T, a_width // n_heads, axis=1)
    row = lambda v: v.reshape(1, -1)
    cur = pl.BlockSpec((None, ts, d), lambda b, s: (b, jnp.minimum(s, nblk - 1), 0))
    lag = pl.BlockSpec((None, ts, d), lambda b, s: (b, jnp.maximum(s - 1, 0), 0))
    params, specs = _resident([row(gmix), w_in, w_out, row(sgain), w_s, bias_full, pool_w,
                               row(pool_scale), row(gnorm), w_gate, w_up, w_down])
    return pl.pallas_call(
        functools.partial(_ab_ffn_kernel, ts=ts, a_width=a_width),
        grid=(bsz, nblk + 1),
        in_specs=[cur] + specs,
        out_specs=lag,
        out_shape=jax.ShapeDtypeStruct(h.shape, F32),
        scratch_shapes=[pltpu.VMEM((ts + POOL_HALO, b_width), F32), pltpu.VMEM((ts, d), F32),
                        pltpu.VMEM((ts, d), F32)],
        compiler_params=pltpu.CompilerParams(
            dimension_semantics=("arbitrary", "arbitrary"), vmem_limit_bytes=VMEM_LIMIT),
        name="ab_ffn",
    )(h, *params)


def _swiglu(h, gn_ref, wg_ref, wu_ref, wd_ref, fin_ref, between=()):
    between = list(between) + [None] * 3
    hn = _rms(h, gn_ref[...]).astype(BF16)
    gate = jnp.dot(hn, wg_ref[...], preferred_element_type=F32)
    if between[0]:
        between[0]()
    up = jnp.dot(hn, wu_ref[...], preferred_element_type=F32)
    if between[1]:
        between[1]()
    mid = gate * jax.nn.sigmoid(gate) * up
    out = h + _bdot(mid, wd_ref[...])
    if between[2]:
        between[2]()
    if fin_ref is not None:
        out = _rms(out, fin_ref[...])
    return out


def _ffn_post_kernel(x_ref, y_ref, g_ref, bonus_ref, lnw_ref, lnb_ref, wo_ref, gn_ref, wg_ref,
                     wu_ref, wd_ref, *rest, tm):
    *fin, o_ref, buf_a, buf_b = rest
    s = pl.program_id(0)

    @pl.when(s == 0)
    def _():
        buf_b[...] = jnp.zeros(buf_b.shape, BF16)

    def prepare(wr, rs):
        y = y_ref[rs, :]
        mean = _head_sum(y) * (1.0 / HEAD_DIM)
        dev = y - mean
        var = _head_sum(dev * dev) * (1.0 / HEAD_DIM)
        yn = dev * lax.rsqrt(var + GN_EPS) * lnw_ref[...] + lnb_ref[...]
        wr[rs, :] = ((yn + bonus_ref[rs, :]) * g_ref[rs, :]).astype(BF16)

    def step(rd, wr):
        quarter = tm // 4
        pieces = [functools.partial(prepare, wr, slice(q * quarter, (q + 1) * quarter))
                  for q in range(4)]
        h = x_ref[...] + jnp.dot(rd[...], wo_ref[...], preferred_element_type=F32)
        pieces[0]()
        o_ref[...] = _swiglu(h, gn_ref, wg_ref, wu_ref, wd_ref, fin[0] if fin else None,
                             between=pieces[1:])

    @pl.when(s % 2 == 0)
    def _():
        step(buf_b, buf_a)

    @pl.when(s % 2 == 1)
    def _():
        step(buf_a, buf_b)


def _ffn_post(h, gnorm, w_gate, w_up, w_down, post, *, tm, final_norm=None):
    bsz, seq, d = h.shape
    n = bsz * seq
    nblk = n // tm
    row = lambda v: v.reshape(1, -1)
    y, g, bonus, ln_w, ln_b, w_o = post
    params = [row(ln_w), row(ln_b), w_o, row(gnorm), w_gate, w_up, w_down]
    if final_norm is not None:
        params.append(row(final_norm))
    params, specs = _resident(params)
    cur = pl.BlockSpec((tm, d), lambda i: (jnp.minimum(i, nblk - 1), 0))
    lag = pl.BlockSpec((tm, d), lambda i: (jnp.maximum(i - 1, 0), 0))
    out = pl.pallas_call(
        functools.partial(_ffn_post_kernel, tm=tm), grid=(nblk + 1,),
        in_specs=[lag, cur, cur, cur] + specs,
        out_specs=lag, out_shape=jax.ShapeDtypeStruct((n, d), F32),
        scratch_shapes=[pltpu.VMEM((tm, d), BF16), pltpu.VMEM((tm, d), BF16)],
        compiler_params=pltpu.CompilerParams(
            dimension_semantics=("arbitrary",), vmem_limit_bytes=VMEM_LIMIT),
        name="ffn_post",
    )(h.reshape(n, d), y.reshape(n, d), g.reshape(n, d), bonus.reshape(n, d), *params)
    return out.reshape(bsz, seq, d)


def _sigmoid(x):
    return 0.5 * jnp.tanh(0.5 * x) + 0.5


def _rwkv_pre_kernel(*refs, ts, vres):
    it = iter(refs)
    (x_ref, gmix_ref, mu_ref, wr_ref, wk_ref, wv_ref, w0_ref, w1_ref, w2_ref, a0_ref, a1_ref,
     a2_ref, g1_ref, g2_ref, kk_ref, ka_ref, rk_ref) = (next(it) for _ in range(17))
    if vres:
        vf_ref, v0_ref, v1_ref, v2_ref = (next(it) for _ in range(4))
    (r_ref, k_ref, nkk_ref, kka_ref, vb_ref, lw_ref, cum_ref, g_ref,
     bonus_ref) = (next(it) for _ in range(9))
    if not vres:
        vf_out_ref = next(it)
    prev_ref, buf_a, buf_b = next(it), next(it), next(it)

    s = pl.program_id(1)

    @pl.when(s == 0)
    def _():
        prev_ref[...] = jnp.zeros(prev_ref.shape, F32)
        buf_b[...] = jnp.zeros(buf_b.shape, F32)

    dot = lambda a, b_ref: jnp.dot(a, b_ref[...], preferred_element_type=F32)

    def step(rd, wr):
        hn = _rms(x_ref[...], gmix_ref[...])
        first = lax.broadcasted_iota(jnp.int32, (ts, 1), 0) == 0
        hprev = jnp.where(first, prev_ref[0:1, :], pltpu.roll(hn, 1, axis=0))
        prev_ref[0:1, :] = hn[ts - 1:ts, :]
        hn_b = hn.astype(BF16)
        xx_b = (hprev - hn).astype(BF16)
        mu_b = mu_ref[...].astype(BF16)
        xr, xw, xk, xv, xa, xg = (hn_b + xx_b * mu_b[i:i + 1, :] for i in range(6))
        hw, ha, hg = dot(xw, w1_ref), dot(xa, a1_ref), dot(xg, g1_ref)
        if vres:
            hv = dot(xv, v1_ref)

        lw = -math.exp(-0.5) * _sigmoid(w0_ref[...] + rd[3])
        lw_ref[...] = lw
        hi = lw.astype(BF16)
        lo = (lw - hi.astype(F32)).astype(BF16)
        ri = lax.broadcasted_iota(jnp.int32, (ts, ts), 0)
        ci = lax.broadcasted_iota(jnp.int32, (ts, ts), 1)
        tri = jnp.where((ri // WKV_CHUNK == ci // WKV_CHUNK) & (ri >= ci), 1.0, 0.0).astype(BF16)

        def tail(rs):
            r, k, v = rd[0, rs, :], rd[1, rs, :], rd[2, rs, :]
            a = _sigmoid(a0_ref[...] + rd[4, rs, :])
            g_ref[rs, :] = rd[5, rs, :]
            if vres:
                v = v + (vf_ref[rs, :] - v) * _sigmoid(v0_ref[...] + rd[6, rs, :])
            else:
                vf_out_ref[rs, :] = v
            vb_ref[rs, :] = v.astype(BF16)
            kk = k * kk_ref[...]
            kk = kk * lax.rsqrt(jnp.maximum(_head_sum(kk * kk), 1e-24))
            ka = ka_ref[...]
            k = k * ((1.0 - ka) + a * ka)
            bonus_ref[rs, :] = _head_sum(r * k * rk_ref[...]) * v
            r_ref[rs, :] = r.astype(BF16)
            k_ref[rs, :] = k.astype(BF16)
            nkk_ref[rs, :] = (-kk).astype(BF16)
            kka_ref[rs, :] = (kk * a).astype(BF16)

        quarter = ts // 4
        wr[0] = dot(xr, wr_ref)
        c2 = jnp.dot(tri, jnp.concatenate([hi, lo], axis=1), preferred_element_type=F32)
        d = lw.shape[1]
        cum_ref[...] = c2[:, :d] + c2[:, d:]
        tail(slice(0, quarter))
        wr[1] = dot(xk, wk_ref)
        tail(slice(quarter, 2 * quarter))
        wr[2] = dot(xv, wv_ref)
        tail(slice(2 * quarter, 3 * quarter))
        wr[3] = dot(jnp.tanh(hw).astype(BF16), w2_ref)
        wr[4] = dot(ha.astype(BF16), a2_ref)
        wr[5] = dot(_sigmoid(hg).astype(BF16), g2_ref)
        if vres:
            wr[6] = dot(hv.astype(BF16), v2_ref)
        tail(slice(3 * quarter, ts))

    @pl.when(s % 2 == 0)
    def _():
        step(buf_b, buf_a)

    @pl.when(s % 2 == 1)
    def _():
        step(buf_a, buf_b)


def _rwkv_pre(h, gmix, p, v_first, vres, *, ts):
    bsz, seq, d = h.shape
    nblk = seq // ts
    row = lambda v: v.reshape(1, -1)
    cur = pl.BlockSpec((None, ts, d), lambda b, s: (b, jnp.minimum(s, nblk - 1), 0))
    lag = pl.BlockSpec((None, ts, d), lambda b, s: (b, jnp.maximum(s - 1, 0), 0))
    params, specs = _resident(
        [row(gmix), p["mu"], p["w_r"], p["w_k"], p["w_v"], row(p["w0"]), p["w1"], p["w2"],
         row(p["a0"]), p["a1"], p["a2"], p["g1"], p["g2"], row(p["k_k"]), row(p["k_a"]),
         row(p["r_k"])])
    args, specs = [h] + params, [cur] + specs
    if vres is not None:
        v0, v1, v2 = vres
        extra, extra_specs = _resident([row(v0), v1, v2])
        args += [v_first] + extra
        specs += [lag] + extra_specs
    tok_bf = jax.ShapeDtypeStruct(h.shape, BF16)
    tok_f32 = jax.ShapeDtypeStruct(h.shape, F32)
    out_shape = [tok_bf] * 5 + [tok_f32] * (4 if vres is not None else 5)
    nproj = 7 if vres is not None else 6
    outs = pl.pallas_call(
        functools.partial(_rwkv_pre_kernel, ts=ts, vres=vres is not None),
        grid=(bsz, nblk + 1),
        in_specs=specs,
        out_specs=[lag] * len(out_shape),
        out_shape=out_shape,
        scratch_shapes=[pltpu.VMEM((8, d), F32), pltpu.VMEM((nproj, ts, d), F32),
                        pltpu.VMEM((nproj, ts, d), F32)],
        compiler_params=pltpu.CompilerParams(
            dimension_semantics=("arbitrary", "arbitrary"), vmem_limit_bytes=VMEM_LIMIT),
        name="rwkv_pre_vres" if vres is not None else "rwkv_pre",
    )(*args)
    scan_in, gate, bonus = outs[:7], outs[7], outs[8]
    if vres is None:
        v_first = outs[9]
    return scan_in, gate, bonus, v_first


def _wkv_kernel(r_ref, k_ref, nkk_ref, kka_ref, v_ref, lw_ref, cum_ref, y_ref, s_ref,
                *, nbatch, npair, nchunk):
    ch = WKV_CHUNK
    pair = 2 * ch

    @pl.when(pl.program_id(1) == 0)
    def _():
        s_ref[...] = jnp.zeros(s_ref.shape, F32)

    row = lax.broadcasted_iota(jnp.int32, (pair, pair), 0)
    col = lax.broadcasted_iota(jnp.int32, (pair, pair), 1)
    same = (row // ch) == (col // ch)
    same_bf = jnp.where(same, 1.0, 0.0).astype(BF16)
    trow = lax.broadcasted_iota(jnp.int32, (ch, pair), 0)
    tcol = lax.broadcasted_iota(jnp.int32, (ch, pair), 1) % ch
    strict = trow > tcol
    incl = trow >= tcol
    eye = jnp.where(trow == tcol, 1.0, 0.0).astype(F32)
    nt = (((1,), (1,)), ((), ()))
    tn = (((0,), (0,)), ((), ()))
    dot = functools.partial(jnp.dot, preferred_element_type=F32)

    def stack(x):
        return jnp.concatenate([x, x], axis=0) * same_bf

    nchain = nbatch * npair
    items = [(ck, q) for ck in range(nchunk) for q in range(nchain)]
    sl = {(ck, q): (q // npair, slice(ck * ch, (ck + 1) * ch),
                    slice((q % npair) * LANES, (q % npair + 1) * LANES)) for ck, q in items}
    ar_lhs, v_bd, a_k, a_rb, power, tinv, bk_end, g_end = {}, {}, {}, {}, {}, {}, {}, {}
    av = {}
    for ck in range(nchunk):
        group = [(ck, q) for q in range(nchain)]
        for it in group:
            bi, rs, ls = sl[it]
            cum = cum_ref[bi, rs, ls]
            g_row = jnp.exp(cum[ch - 1:ch, :])
            g_end[it] = jnp.transpose(jnp.broadcast_to(g_row, (8, LANES)))[:, :1]
            inv = jnp.exp(-cum)
            inv_b = inv.astype(BF16)
            end_b = (inv * g_row).astype(BF16)
            k, kka = k_ref[bi, rs, ls], kka_ref[bi, rs, ls]
            bk_end[it] = jnp.concatenate([kka * end_b, k * end_b], axis=0)
            v_bd[it] = stack(v_ref[bi, rs, ls])
            ar_lhs[it] = jnp.concatenate(
                [nkk_ref[bi, rs, ls] * jnp.exp(cum - lw_ref[bi, rs, ls]).astype(BF16),
                 r_ref[bi, rs, ls] * jnp.exp(cum).astype(BF16)], axis=0)
            rhs = jnp.concatenate([stack(kka * inv_b), stack(k * inv_b)], axis=0)
            amat = dot(ar_lhs[it], rhs.T)
            power[it] = jnp.where(strict, amat[:ch, :pair], 0.0)
            tinv[it] = eye + power[it]
            a_rb[it] = jnp.where(incl, amat[ch:, :pair], 0.0).astype(BF16)
            a_k[it] = jnp.concatenate([jnp.where(strict, amat[:ch, pair:], 0.0),
                                       jnp.where(incl, amat[ch:, pair:], 0.0)],
                                      axis=0).astype(BF16)

        for it in group:
            pb = power[it].astype(BF16)
            power[it] = dot(pb, stack(pb))
        for _ in range(int(math.log2(ch)) - 2):
            for it in group:
                pb = power[it].astype(BF16)
                both = dot(jnp.concatenate([pb, tinv[it].astype(BF16)], axis=0), stack(pb))
                power[it] = both[:ch]
                tinv[it] = tinv[it] + both[ch:]
        for it in group:
            tinv[it] = (tinv[it] + dot(tinv[it].astype(BF16),
                                       stack(power[it].astype(BF16)))).astype(BF16)
        for it in group:
            av[it] = dot(a_k[it], v_bd[it])

    state = [s_ref[q] for q in range(nchain)]
    for ck in range(nchunk):
        ar, u = {}, {}
        for q in range(nchain):
            ar[q] = dot(ar_lhs[ck, q], state[q].astype(BF16)) + av[ck, q]
        for q in range(nchain):
            u[q] = dot(tinv[ck, q], stack(ar[q][:ch].astype(BF16))).astype(BF16)
        for q in range(nchain):
            bi, rs, ls = sl[ck, q]
            y_ref[bi, rs, ls] = ar[q][ch:] + dot(a_rb[ck, q], stack(u[q]))
            upd = lax.dot_general(bk_end[ck, q],
                                  jnp.concatenate([u[q], v_ref[bi, rs, ls]], axis=0),
                                  tn, preferred_element_type=F32)
            state[q] = state[q] * g_end[ck, q] + jnp.where(same, upd, 0.0)
    for q in range(nchain):
        s_ref[q] = state[q]


def _wkv(scan_in, *, npair, nchunk):
    bsz, seq, d = scan_in[0].shape
    width = npair * LANES
    rows = nchunk * WKV_CHUNK
    tok = pl.BlockSpec((bsz, rows, width), lambda p, c: (0, c, p))
    return pl.pallas_call(
        functools.partial(_wkv_kernel, nbatch=bsz, npair=npair, nchunk=nchunk),
        grid=(d // width, seq // rows),
        in_specs=[tok] * len(scan_in),
        out_specs=tok,
        out_shape=jax.ShapeDtypeStruct((bsz, seq, d), F32),
        scratch_shapes=[pltpu.VMEM((bsz * npair, LANES, LANES), F32)],
        compiler_params=pltpu.CompilerParams(
            dimension_semantics=("arbitrary", "arbitrary"), vmem_limit_bytes=VMEM_LIMIT),
        name="wkv",
    )(*scan_in)


def kernel(x, mix_norm, ffn_norm, final_norm, ab_w_in, ab_w_out, sgu_gain, sgu_w_s, sgu_bias, pool_w, pool_scale, rwkv_mu, rwkv_w_r, rwkv_w_k, rwkv_w_v, rwkv_w_o, rwkv_w0, rwkv_w1, rwkv_w2, rwkv_a0, rwkv_a1, rwkv_a2, rwkv_g1, rwkv_g2, rwkv_k_k, rwkv_k_a, rwkv_r_k, rwkv_ln_w, rwkv_ln_b, rwkv_v0, rwkv_v1, rwkv_v2, ffn_w_gate, ffn_w_up, ffn_w_down):
    depth = mix_norm.shape[0]
    assert depth % 2 == 0, "the final norm is fused into the last (RWKV) layer's FFN"
    seq = x.shape[1]
    ts_ab = min(512, seq)
    ts_pre = min(256, seq)
    tm = min(512, seq)
    bf = lambda a: a.astype(BF16)
    ab_w_in, ab_w_out, pool_w = bf(ab_w_in), bf(ab_w_out), bf(pool_w)
    ffn_w_gate, ffn_w_up, ffn_w_down = bf(ffn_w_gate), bf(ffn_w_up), bf(ffn_w_down)
    rwkv_mats = {n: bf(a) for n, a in dict(
        w_r=rwkv_w_r, w_k=rwkv_w_k, w_v=rwkv_w_v, w1=rwkv_w1, w2=rwkv_w2, a1=rwkv_a1,
        a2=rwkv_a2, g1=rwkv_g1, g2=rwkv_g2).items()}
    rwkv_w_o, rwkv_v1, rwkv_v2 = bf(rwkv_w_o), bf(rwkv_v1), bf(rwkv_v2)

    h = x
    v_first = None
    for layer in range(depth):
        i = layer // 2
        ffn_w = [_Stacked(w, layer) for w in (ffn_w_gate, ffn_w_up, ffn_w_down)]
        if layer % 2 == 0:
            h = _ab_ffn(h, mix_norm[layer], _Stacked(ab_w_in, i), _Stacked(ab_w_out, i),
                        sgu_gain[i], _Stacked(sgu_w_s, i), sgu_bias[i], _Stacked(pool_w, i),
                        pool_scale[i], ffn_norm[layer], *ffn_w, ts=ts_ab)
        else:
            p = dict(mu=rwkv_mu[i], w0=rwkv_w0[i], a0=rwkv_a0[i], k_k=rwkv_k_k[i],
                     k_a=rwkv_k_a[i], r_k=rwkv_r_k[i],
                     **{n: _Stacked(a, i) for n, a in rwkv_mats.items()})
            vres = None if i == 0 else (rwkv_v0[i - 1], _Stacked(rwkv_v1, i - 1),
                                        _Stacked(rwkv_v2, i - 1))
            scan_in, gate, bonus, v_first = _rwkv_pre(h, mix_norm[layer], p, v_first, vres,
                                                       ts=ts_pre)
            y = _wkv(scan_in, npair=8, nchunk=4)
            post = (y, gate, bonus, rwkv_ln_w[i], rwkv_ln_b[i], _Stacked(rwkv_w_o, i))
            h = _ffn_post(h, ffn_norm[layer], *ffn_w, post, tm=tm,
                          final_norm=final_norm if layer == depth - 1 else None)
    return h
```

```python
import functools
import math

import jax
import jax.numpy as jnp
from jax import lax
from jax.experimental import pallas as pl
from jax.experimental.pallas import tpu as pltpu

F32 = jnp.float32
BF16 = jnp.bfloat16

EPS = 1e-6
GN_EPS = 64e-5
LANES = 128
HEAD_DIM = 64
WKV_CHUNK = 64
GMLP_BLOCK = 128
GMLP_CAUSAL_CHUNK = 64
POOL_WINDOWS = (2, 4, 8, 16)
POOL_HALO = 16
VMEM_LIMIT = 56 * 1024 * 1024
ROWS_AB_FFN = 512
ROWS_RWKV_PRE = 256
ROWS_FFN_POST = 512
WKV_PAIRS_PER_STEP = 8
WKV_CHUNKS_PER_STEP = 4


def _const_spec(shape):
    nd = len(shape)
    return pl.BlockSpec(shape, lambda *_: (0,) * nd, pipeline_mode=pl.Buffered(1))


class _Stacked:
    def __init__(self, stack, index):
        self.stack, self.index = stack, index


def _resident(params):
    arrays, specs = [], []
    for a in params:
        if isinstance(a, _Stacked):
            nd, i = a.stack.ndim - 1, a.index
            arrays.append(a.stack)
            specs.append(pl.BlockSpec((None,) + a.stack.shape[1:],
                                      lambda *_, i=i, nd=nd: (i,) + (0,) * nd,
                                      pipeline_mode=pl.Buffered(1)))
        else:
            arrays.append(a)
            specs.append(_const_spec(a.shape))
    return arrays, specs


def _rms(x, g):
    return x * lax.rsqrt(jnp.mean(x * x, axis=-1, keepdims=True) + EPS) * g


def _bdot(a, b):
    return jnp.dot(a.astype(BF16), b.astype(BF16), preferred_element_type=F32)


def _head_sum(x):
    lane_lo = lax.broadcasted_iota(jnp.int32, (1, LANES), 1) < HEAD_DIM
    outs = []
    for j in range(x.shape[1] // LANES):
        blk = x[:, j * LANES:(j + 1) * LANES]
        lo = jnp.sum(jnp.where(lane_lo, blk, 0.0), axis=-1, keepdims=True)
        tot = jnp.sum(blk, axis=-1, keepdims=True)
        outs.append(jnp.where(lane_lo, lo, tot - lo))
    return jnp.concatenate(outs, axis=1)


def _ab_body(x_ref, gmix_ref, win_ref, wout_ref, sgain_ref, ws_ref, sbias_ref, pw_ref, pscale_ref,
             dst_ref, zs_ref, *, s, ts, a_width, between=()):
    between = list(between) + [None] * 3
    half = ts // 2
    dot = functools.partial(jnp.dot, preferred_element_type=F32)
    gelu = lambda z: 0.5 * z * (1.0 + lax.erf(z * (1.0 / math.sqrt(2.0))))

    ri = lax.broadcasted_iota(jnp.int32, (GMLP_BLOCK, GMLP_BLOCK), 0) // GMLP_CAUSAL_CHUNK
    ci = lax.broadcasted_iota(jnp.int32, (GMLP_BLOCK, GMLP_BLOCK), 1) // GMLP_CAUSAL_CHUNK
    keep = ri >= ci
    w_s = [jnp.where(keep, ws_ref[hd], 0.0).astype(BF16) for hd in range(a_width // LANES)]

    def in_proj(r0):
        h = x_ref[r0:r0 + half, :]
        hn = _rms(h, gmix_ref[...]).astype(BF16)
        return (h, dot(hn, win_ref[:, :a_width]), dot(hn, win_ref[:, a_width:2 * a_width]),
                dot(hn, win_ref[:, 2 * a_width:]))

    def gate_inputs(zu, zv):
        return gelu(zu), _rms(gelu(zv), sgain_ref[...]).astype(BF16)

    def spatial_gate(u, vn):
        heads = []
        for hd, w in enumerate(w_s):
            cs = slice(hd * LANES, (hd + 1) * LANES)
            bias = sbias_ref[:, cs]
            rows = [dot(w, vn[b0:b0 + GMLP_BLOCK, cs]) + bias
                    for b0 in range(0, half, GMLP_BLOCK)]
            heads.append(jnp.concatenate(rows, axis=0))
        return u * jnp.concatenate(heads, axis=1)

    def pool(r0, zb):
        base = POOL_HALO + r0
        zs_ref[base:base + half, :] = zb
        t1 = (lax.broadcasted_iota(jnp.int32, (half, 1), 0) + (s * ts + r0 + 1)).astype(F32)
        groups = []
        for gi, win in enumerate(POOL_WINDOWS):
            cs = slice(gi * LANES, (gi + 1) * LANES)
            cur = zb[:, cs]
            acc = cur
            for j in range(1, win):
                acc = acc + zs_ref[pl.ds(base - j, half), cs]
            pooled = acc / jnp.minimum(t1, float(win)) - cur
            groups.append(_bdot(pooled, pw_ref[gi]))
        return jnp.concatenate(groups, axis=1) * pscale_ref[...]

    def out_proj(r0, h, y_a, y_b):
        cat = jnp.concatenate([y_a, y_b], axis=1)
        dst_ref[r0:r0 + half, :] = h + _bdot(cat, wout_ref[...])

    h0, zu0, zv0, zp0 = in_proj(0)
    h1, zu1, zv1, zp1 = in_proj(half)
    if between[0]:
        between[0]()
    u0, vn0 = gate_inputs(zu0, zv0)
    ya0 = spatial_gate(u0, vn0)
    yb0 = pool(0, zp0)
    u1, vn1 = gate_inputs(zu1, zv1)
    out_proj(0, h0, ya0, yb0)
    if between[1]:
        between[1]()
    ya1 = spatial_gate(u1, vn1)
    yb1 = pool(half, zp1)
    out_proj(half, h1, ya1, yb1)
    zs_ref[0:POOL_HALO, :] = zs_ref[ts:ts + POOL_HALO, :]
    if between[2]:
        between[2]()


def _ab_ffn_kernel(x_ref, gmix_ref, win_ref, wout_ref, sgain_ref, ws_ref, sbias_ref, pw_ref,
                   pscale_ref, gn_ref, wg_ref, wu_ref, wd_ref, o_ref, zs_ref, buf_a, buf_b,
                   *, ts, a_width):
    s = pl.program_id(1)

    @pl.when(s == 0)
    def _():
        zs_ref[0:POOL_HALO, :] = jnp.zeros((POOL_HALO, zs_ref.shape[1]), F32)
        buf_b[...] = jnp.zeros(buf_b.shape, F32)

    def step(rd, wr):
        ffn = {}

        def gate():
            ffn["h"] = rd[...]
            ffn["hn"] = _rms(ffn["h"], gn_ref[...]).astype(BF16)
            ffn["gate"] = jnp.dot(ffn["hn"], wg_ref[...], preferred_element_type=F32)

        def up():
            ffn["up"] = jnp.dot(ffn["hn"], wu_ref[...], preferred_element_type=F32)

        def down():
            mid = ffn["gate"] * jax.nn.sigmoid(ffn["gate"]) * ffn["up"]
            o_ref[...] = ffn["h"] + _bdot(mid, wd_ref[...])

        _ab_body(x_ref, gmix_ref, win_ref, wout_ref, sgain_ref, ws_ref, sbias_ref, pw_ref,
                 pscale_ref, wr, zs_ref, s=s, ts=ts, a_width=a_width, between=(gate, up, down))

    @pl.when(s % 2 == 0)
    def _():
        step(buf_b, buf_a)

    @pl.when(s % 2 == 1)
    def _():
        step(buf_a, buf_b)


def _ab_ffn(h, gmix, w_in, w_out, sgain, w_s, s_bias, pool_w, pool_scale, gnorm, w_gate, w_up,
            w_down, *, ts):
    bsz, seq, d = h.shape
    nblk = seq // ts
    a_width = sgain.shape[0]
    b_width = pool_scale.shape[0]
    n_heads = s_bias.shape[0]
    bias_full = jnp.repeat(s_bias.T, a_width // n_heads, axis=1)
    row = lambda v: v.reshape(1, -1)
    cur = pl.BlockSpec((None, ts, d), lambda b, s: (b, jnp.minimum(s, nblk - 1), 0))
    lag = pl.BlockSpec((None, ts, d), lambda b, s: (b, jnp.maximum(s - 1, 0), 0))
    params, specs = _resident([row(gmix), w_in, w_out, row(sgain), w_s, bias_full, pool_w,
                               row(pool_scale), row(gnorm), w_gate, w_up, w_down])
    return pl.pallas_call(
        functools.partial(_ab_ffn_kernel, ts=ts, a_width=a_width),
        grid=(bsz, nblk + 1),
        in_specs=[cur] + specs,
        out_specs=lag,
        out_shape=jax.ShapeDtypeStruct(h.shape, F32),
        scratch_shapes=[pltpu.VMEM((ts + POOL_HALO, b_width), F32), pltpu.VMEM((ts, d), F32),
                        pltpu.VMEM((ts, d), F32)],
        compiler_params=pltpu.CompilerParams(
            dimension_semantics=("arbitrary", "arbitrary"), vmem_limit_bytes=VMEM_LIMIT),
        name="ab_ffn",
    )(h, *params)


def _swiglu(h, gn_ref, wg_ref, wu_ref, wd_ref, fin_ref, between=()):
    between = list(between) + [None] * 3
    hn = _rms(h, gn_ref[...]).astype(BF16)
    gate = jnp.dot(hn, wg_ref[...], preferred_element_type=F32)
    if between[0]:
        between[0]()
    up = jnp.dot(hn, wu_ref[...], preferred_element_type=F32)
    if between[1]:
        between[1]()
    mid = gate * jax.nn.sigmoid(gate) * up
    out = h + _bdot(mid, wd_ref[...])
    if between[2]:
        between[2]()
    if fin_ref is not None:
        out = _rms(out, fin_ref[...])
    return out


def _ffn_post_kernel(x_ref, y_ref, g_ref, bonus_ref, lnw_ref, lnb_ref, wo_ref, gn_ref, wg_ref,
                     wu_ref, wd_ref, *rest, tm):
    *fin, o_ref, buf_a, buf_b = rest
    s = pl.program_id(0)

    @pl.when(s == 0)
    def _():
        buf_b[...] = jnp.zeros(buf_b.shape, BF16)

    def prepare(wr, rs):
        y = y_ref[rs, :]
        mean = _head_sum(y) * (1.0 / HEAD_DIM)
        dev = y - mean
        var = _head_sum(dev * dev) * (1.0 / HEAD_DIM)
        yn = dev * lax.rsqrt(var + GN_EPS) * lnw_ref[...] + lnb_ref[...]
        wr[rs, :] = ((yn + bonus_ref[rs, :]) * g_ref[rs, :]).astype(BF16)

    def step(rd, wr):
        quarter = tm // 4
        pieces = [functools.partial(prepare, wr, slice(q * quarter, (q + 1) * quarter))
                  for q in range(4)]
        h = x_ref[...] + jnp.dot(rd[...], wo_ref[...], preferred_element_type=F32)
        pieces[0]()
        o_ref[...] = _swiglu(h, gn_ref, wg_ref, wu_ref, wd_ref, fin[0] if fin else None,
                             between=pieces[1:])

    @pl.when(s % 2 == 0)
    def _():
        step(buf_b, buf_a)

    @pl.when(s % 2 == 1)
    def _():
        step(buf_a, buf_b)


def _ffn_post(h, gnorm, w_gate, w_up, w_down, post, *, tm, final_norm=None):
    bsz, seq, d = h.shape
    n = bsz * seq
    nblk = n // tm
    row = lambda v: v.reshape(1, -1)
    y, g, bonus, ln_w, ln_b, w_o = post
    params = [row(ln_w), row(ln_b), w_o, row(gnorm), w_gate, w_up, w_down]
    if final_norm is not None:
        params.append(row(final_norm))
    params, specs = _resident(params)
    cur = pl.BlockSpec((tm, d), lambda i: (jnp.minimum(i, nblk - 1), 0))
    lag = pl.BlockSpec((tm, d), lambda i: (jnp.maximum(i - 1, 0), 0))
    out = pl.pallas_call(
        functools.partial(_ffn_post_kernel, tm=tm), grid=(nblk + 1,),
        in_specs=[lag, cur, cur, cur] + specs,
        out_specs=lag, out_shape=jax.ShapeDtypeStruct((n, d), F32),
        scratch_shapes=[pltpu.VMEM((tm, d), BF16), pltpu.VMEM((tm, d), BF16)],
        compiler_params=pltpu.CompilerParams(
            dimension_semantics=("arbitrary",), vmem_limit_bytes=VMEM_LIMIT),
        name="ffn_post",
    )(h.reshape(n, d), y.reshape(n, d), g.reshape(n, d), bonus.reshape(n, d), *params)
    return out.reshape(bsz, seq, d)


def _sigmoid(x):
    return 0.5 * jnp.tanh(0.5 * x) + 0.5


def _rwkv_pre_kernel(*refs, ts, vres):
    it = iter(refs)
    (x_ref, gmix_ref, mu_ref, wr_ref, wk_ref, wv_ref, w0_ref, w1_ref, w2_ref, a0_ref, a1_ref,
     a2_ref, g1_ref, g2_ref, kk_ref, ka_ref, rk_ref) = (next(it) for _ in range(17))
    if vres:
        vf_ref, v0_ref, v1_ref, v2_ref = (next(it) for _ in range(4))
    (r_ref, k_ref, nkk_ref, kka_ref, vb_ref, lw_ref, cum_ref, g_ref,
     bonus_ref) = (next(it) for _ in range(9))
    if not vres:
        vf_out_ref = next(it)
    prev_ref, buf_a, buf_b = next(it), next(it), next(it)

    s = pl.program_id(1)

    @pl.when(s == 0)
    def _():
        prev_ref[...] = jnp.zeros(prev_ref.shape, F32)
        buf_b[...] = jnp.zeros(buf_b.shape, F32)

    dot = lambda a, b_ref: jnp.dot(a, b_ref[...], preferred_element_type=F32)

    def step(rd, wr):
        hn = _rms(x_ref[...], gmix_ref[...])
        first = lax.broadcasted_iota(jnp.int32, (ts, 1), 0) == 0
        hprev = jnp.where(first, prev_ref[0:1, :], pltpu.roll(hn, 1, axis=0))
        prev_ref[0:1, :] = hn[ts - 1:ts, :]
        hn_b = hn.astype(BF16)
        xx_b = (hprev - hn).astype(BF16)
        mu_b = mu_ref[...].astype(BF16)
        xr, xw, xk, xv, xa, xg = (hn_b + xx_b * mu_b[i:i + 1, :] for i in range(6))
        hw, ha, hg = dot(xw, w1_ref), dot(xa, a1_ref), dot(xg, g1_ref)
        if vres:
            hv = dot(xv, v1_ref)

        lw = -math.exp(-0.5) * _sigmoid(w0_ref[...] + rd[3])
        lw_ref[...] = lw
        hi = lw.astype(BF16)
        lo = (lw - hi.astype(F32)).astype(BF16)
        ri = lax.broadcasted_iota(jnp.int32, (ts, ts), 0)
        ci = lax.broadcasted_iota(jnp.int32, (ts, ts), 1)
        tri = jnp.where((ri // WKV_CHUNK == ci // WKV_CHUNK) & (ri >= ci), 1.0, 0.0).astype(BF16)

        def tail(rs):
            r, k, v = rd[0, rs, :], rd[1, rs, :], rd[2, rs, :]
            a = _sigmoid(a0_ref[...] + rd[4, rs, :])
            g_ref[rs, :] = rd[5, rs, :]
            if vres:
                v = v + (vf_ref[rs, :] - v) * _sigmoid(v0_ref[...] + rd[6, rs, :])
            else:
                vf_out_ref[rs, :] = v
            vb_ref[rs, :] = v.astype(BF16)
            kk = k * kk_ref[...]
            kk = kk * lax.rsqrt(jnp.maximum(_head_sum(kk * kk), 1e-24))
            ka = ka_ref[...]
            k = k * ((1.0 - ka) + a * ka)
            bonus_ref[rs, :] = _head_sum(r * k * rk_ref[...]) * v
            r_ref[rs, :] = r.astype(BF16)
            k_ref[rs, :] = k.astype(BF16)
            nkk_ref[rs, :] = (-kk).astype(BF16)
            kka_ref[rs, :] = (kk * a).astype(BF16)

        quarter = ts // 4
        wr[0] = dot(xr, wr_ref)
        c2 = jnp.dot(tri, jnp.concatenate([hi, lo], axis=1), preferred_element_type=F32)
        d = lw.shape[1]
        cum_ref[...] = c2[:, :d] + c2[:, d:]
        tail(slice(0, quarter))
        wr[1] = dot(xk, wk_ref)
        tail(slice(quarter, 2 * quarter))
        wr[2] = dot(xv, wv_ref)
        tail(slice(2 * quarter, 3 * quarter))
        wr[3] = dot(jnp.tanh(hw).astype(BF16), w2_ref)
        wr[4] = dot(ha.astype(BF16), a2_ref)
        wr[5] = dot(_sigmoid(hg).astype(BF16), g2_ref)
        if vres:
            wr[6] = dot(hv.astype(BF16), v2_ref)
        tail(slice(3 * quarter, ts))

    @pl.when(s % 2 == 0)
    def _():
        step(buf_b, buf_a)

    @pl.when(s % 2 == 1)
    def _():
        step(buf_a, buf_b)


def _rwkv_pre(h, gmix, p, v_first, vres, *, ts):
    bsz, seq, d = h.shape
    nblk = seq // ts
    row = lambda v: v.reshape(1, -1)
    cur = pl.BlockSpec((None, ts, d), lambda b, s: (b, jnp.minimum(s, nblk - 1), 0))
    lag = pl.BlockSpec((None, ts, d), lambda b, s: (b, jnp.maximum(s - 1, 0), 0))
    params, specs = _resident(
        [row(gmix), p["mu"], p["w_r"], p["w_k"], p["w_v"], row(p["w0"]), p["w1"], p["w2"],
         row(p["a0"]), p["a1"], p["a2"], p["g1"], p["g2"], row(p["k_k"]), row(p["k_a"]),
         row(p["r_k"])])
    args, specs = [h] + params, [cur] + specs
    if vres is not None:
        v0, v1, v2 = vres
        extra, extra_specs = _resident([row(v0), v1, v2])
        args += [v_first] + extra
        specs += [lag] + extra_specs
    tok_bf = jax.ShapeDtypeStruct(h.shape, BF16)
    tok_f32 = jax.ShapeDtypeStruct(h.shape, F32)
    out_shape = [tok_bf] * 5 + [tok_f32] * (4 if vres is not None else 5)
    nproj = 7 if vres is not None else 6
    outs = pl.pallas_call(
        functools.partial(_rwkv_pre_kernel, ts=ts, vres=vres is not None),
        grid=(bsz, nblk + 1),
        in_specs=specs,
        out_specs=[lag] * len(out_shape),
        out_shape=out_shape,
        scratch_shapes=[pltpu.VMEM((8, d), F32), pltpu.VMEM((nproj, ts, d), F32),
                        pltpu.VMEM((nproj, ts, d), F32)],
        compiler_params=pltpu.CompilerParams(
            dimension_semantics=("arbitrary", "arbitrary"), vmem_limit_bytes=VMEM_LIMIT),
        name="rwkv_pre_vres" if vres is not None else "rwkv_pre",
    )(*args)
    scan_in, gate, bonus = outs[:7], outs[7], outs[8]
    if vres is None:
        v_first = outs[9]
    return scan_in, gate, bonus, v_first


def _wkv_kernel(r_ref, k_ref, nkk_ref, kka_ref, v_ref, lw_ref, cum_ref, y_ref, s_ref,
                *, nbatch, npair, nchunk):
    ch = WKV_CHUNK
    pair = 2 * ch

    @pl.when(pl.program_id(1) == 0)
    def _():
        s_ref[...] = jnp.zeros(s_ref.shape, F32)

    row = lax.broadcasted_iota(jnp.int32, (pair, pair), 0)
    col = lax.broadcasted_iota(jnp.int32, (pair, pair), 1)
    same = (row // ch) == (col // ch)
    same_bf = jnp.where(same, 1.0, 0.0).astype(BF16)
    trow = lax.broadcasted_iota(jnp.int32, (ch, pair), 0)
    tcol = lax.broadcasted_iota(jnp.int32, (ch, pair), 1) % ch
    strict = trow > tcol
    incl = trow >= tcol
    eye = jnp.where(trow == tcol, 1.0, 0.0).astype(F32)
    tn = (((0,), (0,)), ((), ()))
    dot = functools.partial(jnp.dot, preferred_element_type=F32)

    def stack(x):
        return jnp.concatenate([x, x], axis=0) * same_bf

    nchain = nbatch * npair
    items = [(ck, q) for ck in range(nchunk) for q in range(nchain)]
    sl = {(ck, q): (q // npair, slice(ck * ch, (ck + 1) * ch),
                    slice((q % npair) * LANES, (q % npair + 1) * LANES)) for ck, q in items}
    ar_lhs, v_bd, a_k, a_rb, power, tinv, bk_end, g_end = {}, {}, {}, {}, {}, {}, {}, {}
    av = {}
    for ck in range(nchunk):
        group = [(ck, q) for q in range(nchain)]
        for it in group:
            bi, rs, ls = sl[it]
            cum = cum_ref[bi, rs, ls]
            g_row = jnp.exp(cum[ch - 1:ch, :])
            g_end[it] = jnp.transpose(jnp.broadcast_to(g_row, (8, LANES)))[:, :1]
            inv = jnp.exp(-cum)
            inv_b = inv.astype(BF16)
            end_b = (inv * g_row).astype(BF16)
            k, kka = k_ref[bi, rs, ls], kka_ref[bi, rs, ls]
            bk_end[it] = jnp.concatenate([kka * end_b, k * end_b], axis=0)
            v_bd[it] = stack(v_ref[bi, rs, ls])
            ar_lhs[it] = jnp.concatenate(
                [nkk_ref[bi, rs, ls] * jnp.exp(cum - lw_ref[bi, rs, ls]).astype(BF16),
                 r_ref[bi, rs, ls] * jnp.exp(cum).astype(BF16)], axis=0)
            rhs = jnp.concatenate([stack(kka * inv_b), stack(k * inv_b)], axis=0)
            amat = dot(ar_lhs[it], rhs.T)
            power[it] = jnp.where(strict, amat[:ch, :pair], 0.0)
            tinv[it] = eye + power[it]
            a_rb[it] = jnp.where(incl, amat[ch:, :pair], 0.0).astype(BF16)
            a_k[it] = jnp.concatenate([jnp.where(strict, amat[:ch, pair:], 0.0),
                                       jnp.where(incl, amat[ch:, pair:], 0.0)],
                                      axis=0).astype(BF16)

        for it in group:
            pb = power[it].astype(BF16)
            power[it] = dot(pb, stack(pb))
        for _ in range(int(math.log2(ch)) - 2):
            for it in group:
                pb = power[it].astype(BF16)
                both = dot(jnp.concatenate([pb, tinv[it].astype(BF16)], axis=0), stack(pb))
                power[it] = both[:ch]
                tinv[it] = tinv[it] + both[ch:]
        for it in group:
            tinv[it] = (tinv[it] + dot(tinv[it].astype(BF16),
                                       stack(power[it].astype(BF16)))).astype(BF16)
        for it in group:
            av[it] = dot(a_k[it], v_bd[it])

    state = [s_ref[q] for q in range(nchain)]
    for ck in range(nchunk):
        ar, u = {}, {}
        for q in range(nchain):
            ar[q] = dot(ar_lhs[ck, q], state[q].astype(BF16)) + av[ck, q]
        for q in range(nchain):
            u[q] = dot(tinv[ck, q], stack(ar[q][:ch].astype(BF16))).astype(BF16)
        for q in range(nchain):
            bi, rs, ls = sl[ck, q]
            y_ref[bi, rs, ls] = ar[q][ch:] + dot(a_rb[ck, q], stack(u[q]))
            upd = lax.dot_general(bk_end[ck, q],
                                  jnp.concatenate([u[q], v_ref[bi, rs, ls]], axis=0),
                                  tn, preferred_element_type=F32)
            state[q] = state[q] * g_end[ck, q] + jnp.where(same, upd, 0.0)
    for q in range(nchain):
        s_ref[q] = state[q]


def _wkv(scan_in, *, npair, nchunk):
    bsz, seq, d = scan_in[0].shape
    width = npair * LANES
    rows = nchunk * WKV_CHUNK
    tok = pl.BlockSpec((bsz, rows, width), lambda p, c: (0, c, p))
    return pl.pallas_call(
        functools.partial(_wkv_kernel, nbatch=bsz, npair=npair, nchunk=nchunk),
        grid=(d // width, seq // rows),
        in_specs=[tok] * len(scan_in),
        out_specs=tok,
        out_shape=jax.ShapeDtypeStruct((bsz, seq, d), F32),
        scratch_shapes=[pltpu.VMEM((bsz * npair, LANES, LANES), F32)],
        compiler_params=pltpu.CompilerParams(
            dimension_semantics=("arbitrary", "arbitrary"), vmem_limit_bytes=VMEM_LIMIT),
        name="wkv",
    )(*scan_in)


def kernel(x, mix_norm, ffn_norm, final_norm, ab_w_in, ab_w_out, sgu_gain, sgu_w_s, sgu_bias, pool_w, pool_scale, rwkv_mu, rwkv_w_r, rwkv_w_k, rwkv_w_v, rwkv_w_o, rwkv_w0, rwkv_w1, rwkv_w2, rwkv_a0, rwkv_a1, rwkv_a2, rwkv_g1, rwkv_g2, rwkv_k_k, rwkv_k_a, rwkv_r_k, rwkv_ln_w, rwkv_ln_b, rwkv_v0, rwkv_v1, rwkv_v2, ffn_w_gate, ffn_w_up, ffn_w_down):
    depth = mix_norm.shape[0]
    assert depth % 2 == 0, "the final norm is fused into the last (RWKV) layer's FFN"
    seq = x.shape[1]
    ts_ab = min(ROWS_AB_FFN, seq)
    ts_pre = min(ROWS_RWKV_PRE, seq)
    tm = min(ROWS_FFN_POST, seq)
    bf = lambda a: a.astype(BF16)
    ab_w_in, ab_w_out, pool_w = bf(ab_w_in), bf(ab_w_out), bf(pool_w)
    ffn_w_gate, ffn_w_up, ffn_w_down = bf(ffn_w_gate), bf(ffn_w_up), bf(ffn_w_down)
    rwkv_mats = {n: bf(a) for n, a in dict(
        w_r=rwkv_w_r, w_k=rwkv_w_k, w_v=rwkv_w_v, w1=rwkv_w1, w2=rwkv_w2, a1=rwkv_a1,
        a2=rwkv_a2, g1=rwkv_g1, g2=rwkv_g2).items()}
    rwkv_w_o, rwkv_v1, rwkv_v2 = bf(rwkv_w_o), bf(rwkv_v1), bf(rwkv_v2)

    h = x
    v_first = None
    for layer in range(depth):
        i = layer // 2
        ffn_w = [_Stacked(w, layer) for w in (ffn_w_gate, ffn_w_up, ffn_w_down)]
        if layer % 2 == 0:
            h = _ab_ffn(h, mix_norm[layer], _Stacked(ab_w_in, i), _Stacked(ab_w_out, i),
                        sgu_gain[i], _Stacked(sgu_w_s, i), sgu_bias[i], _Stacked(pool_w, i),
                        pool_scale[i], ffn_norm[layer], *ffn_w, ts=ts_ab)
        else:
            p = dict(mu=rwkv_mu[i], w0=rwkv_w0[i], a0=rwkv_a0[i], k_k=rwkv_k_k[i],
                     k_a=rwkv_k_a[i], r_k=rwkv_r_k[i],
                     **{n: _Stacked(a, i) for n, a in rwkv_mats.items()})
            vres = None if i == 0 else (rwkv_v0[i - 1], _Stacked(rwkv_v1, i - 1),
                                        _Stacked(rwkv_v2, i - 1))
            scan_in, gate, bonus, v_first = _rwkv_pre(h, mix_norm[layer], p, v_first, vres,
                                                       ts=ts_pre)
            y = _wkv(scan_in, npair=WKV_PAIRS_PER_STEP, nchunk=WKV_CHUNKS_PER_STEP)
            post = (y, gate, bonus, rwkv_ln_w[i], rwkv_ln_b[i], _Stacked(rwkv_w_o, i))
            h = _ffn_post(h, ffn_norm[layer], *ffn_w, post, tm=tm,
                          final_norm=final_norm if layer == depth - 1 else None)
    return h
```

```python
import functools
import math

import jax
import jax.numpy as jnp
from jax import lax
from jax.experimental import pallas as pl
from jax.experimental.pallas import tpu as pltpu

F32 = jnp.float32
BF16 = jnp.bfloat16

EPS = 1e-6
GN_EPS = 64e-5
LANES = 128
HEAD_DIM = 64
WKV_CHUNK = 64
GMLP_BLOCK = 128
GMLP_CAUSAL_CHUNK = 64
POOL_WINDOWS = (2, 4, 8, 16)
POOL_HALO = 16
VMEM_LIMIT = 56 * 1024 * 1024
ROWS_AB_FFN = 512
ROWS_RWKV_PRE = 256
ROWS_FFN_POST = 512
WKV_PAIRS_PER_STEP = 8
WKV_CHUNKS_PER_STEP = 4


def _const_spec(shape):
    nd = len(shape)
    return pl.BlockSpec(shape, lambda *_: (0,) * nd, pipeline_mode=pl.Buffered(1))


class _Stacked:
    def __init__(self, stack, index):
        self.stack, self.index = stack, index


def _resident(params):
    arrays, specs = [], []
    for a in params:
        if isinstance(a, _Stacked):
            nd, i = a.stack.ndim - 1, a.index
            arrays.append(a.stack)
            specs.append(pl.BlockSpec((None,) + a.stack.shape[1:],
                                      lambda *_, i=i, nd=nd: (i,) + (0,) * nd,
                                      pipeline_mode=pl.Buffered(1)))
        else:
            arrays.append(a)
            specs.append(_const_spec(a.shape))
    return arrays, specs


def _two_stage_pipeline(s, nblk, buf_a, buf_b, step):
    @pl.when(s == 0)
    def _():
        step(None, buf_a)

    @pl.when((s > 0) & (s < nblk) & (s % 2 == 0))
    def _():
        step(buf_b, buf_a)

    @pl.when((s < nblk) & (s % 2 == 1))
    def _():
        step(buf_a, buf_b)

    @pl.when(s == nblk)
    def _():
        step((buf_a, buf_b)[(nblk - 1) % 2], None)


def _rms(x, g):
    return x * lax.rsqrt(jnp.mean(x * x, axis=-1, keepdims=True) + EPS) * g


def _bdot(a, b):
    return jnp.dot(a.astype(BF16), b.astype(BF16), preferred_element_type=F32)


def _head_sum(x):
    lane_lo = lax.broadcasted_iota(jnp.int32, (1, LANES), 1) < HEAD_DIM
    outs = []
    for j in range(x.shape[1] // LANES):
        blk = x[:, j * LANES:(j + 1) * LANES]
        lo = jnp.sum(jnp.where(lane_lo, blk, 0.0), axis=-1, keepdims=True)
        tot = jnp.sum(blk, axis=-1, keepdims=True)
        outs.append(jnp.where(lane_lo, lo, tot - lo))
    return jnp.concatenate(outs, axis=1)


def _ab_body(x_ref, gmix_ref, win_ref, wout_ref, sgain_ref, ws_ref, sbias_ref, pw_ref, pscale_ref,
             dst_ref, zs_ref, *, s, ts, a_width, between=()):
    between = list(between) + [None] * 3
    half = ts // 2
    dot = functools.partial(jnp.dot, preferred_element_type=F32)
    gelu = lambda z: 0.5 * z * (1.0 + lax.erf(z * (1.0 / math.sqrt(2.0))))

    ri = lax.broadcasted_iota(jnp.int32, (GMLP_BLOCK, GMLP_BLOCK), 0) // GMLP_CAUSAL_CHUNK
    ci = lax.broadcasted_iota(jnp.int32, (GMLP_BLOCK, GMLP_BLOCK), 1) // GMLP_CAUSAL_CHUNK
    keep = ri >= ci
    w_s = [jnp.where(keep, ws_ref[hd], 0.0).astype(BF16) for hd in range(a_width // LANES)]

    def in_proj(r0):
        h = x_ref[r0:r0 + half, :]
        hn = _rms(h, gmix_ref[...]).astype(BF16)
        return (h, dot(hn, win_ref[:, :a_width]), dot(hn, win_ref[:, a_width:2 * a_width]),
                dot(hn, win_ref[:, 2 * a_width:]))

    def gate_inputs(zu, zv):
        return gelu(zu), _rms(gelu(zv), sgain_ref[...]).astype(BF16)

    def spatial_gate(u, vn):
        heads = []
        for hd, w in enumerate(w_s):
            cs = slice(hd * LANES, (hd + 1) * LANES)
            bias = sbias_ref[:, cs]
            rows = [dot(w, vn[b0:b0 + GMLP_BLOCK, cs]) + bias
                    for b0 in range(0, half, GMLP_BLOCK)]
            heads.append(jnp.concatenate(rows, axis=0))
        return u * jnp.concatenate(heads, axis=1)

    def pool(r0, zb):
        base = POOL_HALO + r0
        zs_ref[base:base + half, :] = zb
        t1 = (lax.broadcasted_iota(jnp.int32, (half, 1), 0) + (s * ts + r0 + 1)).astype(F32)
        groups = []
        for gi, win in enumerate(POOL_WINDOWS):
            cs = slice(gi * LANES, (gi + 1) * LANES)
            cur = zb[:, cs]
            acc = cur
            for j in range(1, win):
                acc = acc + zs_ref[pl.ds(base - j, half), cs]
            pooled = acc / jnp.minimum(t1, float(win)) - cur
            groups.append(_bdot(pooled, pw_ref[gi]))
        return jnp.concatenate(groups, axis=1) * pscale_ref[...]

    def out_proj(r0, h, y_a, y_b):
        cat = jnp.concatenate([y_a, y_b], axis=1)
        dst_ref[r0:r0 + half, :] = h + _bdot(cat, wout_ref[...])

    h0, zu0, zv0, zp0 = in_proj(0)
    h1, zu1, zv1, zp1 = in_proj(half)
    if between[0]:
        between[0]()
    u0, vn0 = gate_inputs(zu0, zv0)
    ya0 = spatial_gate(u0, vn0)
    yb0 = pool(0, zp0)
    u1, vn1 = gate_inputs(zu1, zv1)
    out_proj(0, h0, ya0, yb0)
    if between[1]:
        between[1]()
    ya1 = spatial_gate(u1, vn1)
    yb1 = pool(half, zp1)
    out_proj(half, h1, ya1, yb1)
    zs_ref[0:POOL_HALO, :] = zs_ref[ts:ts + POOL_HALO, :]
    if between[2]:
        between[2]()


def _ab_ffn_kernel(x_ref, gmix_ref, win_ref, wout_ref, sgain_ref, ws_ref, sbias_ref, pw_ref,
                   pscale_ref, gn_ref, wg_ref, wu_ref, wd_ref, o_ref, zs_ref, buf_a, buf_b,
                   *, ts, nblk, a_width):
    s = pl.program_id(1)

    @pl.when(s == 0)
    def _():
        zs_ref[0:POOL_HALO, :] = jnp.zeros((POOL_HALO, zs_ref.shape[1]), F32)

    def step(rd, wr):
        ffn = {}

        def gate():
            ffn["h"] = rd[...]
            ffn["hn"] = _rms(ffn["h"], gn_ref[...]).astype(BF16)
            ffn["gate"] = jnp.dot(ffn["hn"], wg_ref[...], preferred_element_type=F32)

        def up():
            ffn["up"] = jnp.dot(ffn["hn"], wu_ref[...], preferred_element_type=F32)

        def down():
            mid = ffn["gate"] * jax.nn.sigmoid(ffn["gate"]) * ffn["up"]
            o_ref[...] = ffn["h"] + _bdot(mid, wd_ref[...])

        if wr is None:
            gate(), up(), down()
        else:
            _ab_body(x_ref, gmix_ref, win_ref, wout_ref, sgain_ref, ws_ref, sbias_ref, pw_ref,
                     pscale_ref, wr, zs_ref, s=s, ts=ts, a_width=a_width,
                     between=() if rd is None else (gate, up, down))

    _two_stage_pipeline(s, nblk, buf_a, buf_b, step)


def _ab_ffn(h, gmix, w_in, w_out, sgain, w_s, s_bias, pool_w, pool_scale, gnorm, w_gate, w_up,
            w_down, *, ts):
    bsz, seq, d = h.shape
    nblk = seq // ts
    a_width = sgain.shape[0]
    b_width = pool_scale.shape[0]
    n_heads = s_bias.shape[0]
    bias_full = jnp.repeat(s_bias.T, a_width // n_heads, axis=1)
    row = lambda v: v.reshape(1, -1)
    cur = pl.BlockSpec((None, ts, d), lambda b, s: (b, jnp.minimum(s, nblk - 1), 0))
    lag = pl.BlockSpec((None, ts, d), lambda b, s: (b, jnp.maximum(s - 1, 0), 0))
    params, specs = _resident([row(gmix), w_in, w_out, row(sgain), w_s, bias_full, pool_w,
                               row(pool_scale), row(gnorm), w_gate, w_up, w_down])
    return pl.pallas_call(
        functools.partial(_ab_ffn_kernel, ts=ts, nblk=nblk, a_width=a_width),
        grid=(bsz, nblk + 1),
        in_specs=[cur] + specs,
        out_specs=lag,
        out_shape=jax.ShapeDtypeStruct(h.shape, F32),
        scratch_shapes=[pltpu.VMEM((ts + POOL_HALO, b_width), F32), pltpu.VMEM((ts, d), F32),
                        pltpu.VMEM((ts, d), F32)],
        compiler_params=pltpu.CompilerParams(
            dimension_semantics=("arbitrary", "arbitrary"), vmem_limit_bytes=VMEM_LIMIT),
        name="ab_ffn",
    )(h, *params)


def _swiglu(h, gn_ref, wg_ref, wu_ref, wd_ref, fin_ref, between=()):
    between = list(between) + [None] * 3
    hn = _rms(h, gn_ref[...]).astype(BF16)
    gate = jnp.dot(hn, wg_ref[...], preferred_element_type=F32)
    if between[0]:
        between[0]()
    up = jnp.dot(hn, wu_ref[...], preferred_element_type=F32)
    if between[1]:
        between[1]()
    mid = gate * jax.nn.sigmoid(gate) * up
    out = h + _bdot(mid, wd_ref[...])
    if between[2]:
        between[2]()
    if fin_ref is not None:
        out = _rms(out, fin_ref[...])
    return out


def _ffn_post_kernel(x_ref, y_ref, g_ref, bonus_ref, lnw_ref, lnb_ref, wo_ref, gn_ref, wg_ref,
                     wu_ref, wd_ref, *rest, tm, nblk):
    *fin, o_ref, buf_a, buf_b = rest
    s = pl.program_id(0)

    def prepare(wr, rs):
        y = y_ref[rs, :]
        mean = _head_sum(y) * (1.0 / HEAD_DIM)
        dev = y - mean
        var = _head_sum(dev * dev) * (1.0 / HEAD_DIM)
        yn = dev * lax.rsqrt(var + GN_EPS) * lnw_ref[...] + lnb_ref[...]
        wr[rs, :] = ((yn + bonus_ref[rs, :]) * g_ref[rs, :]).astype(BF16)

    def step(rd, wr):
        quarter = tm // 4
        pieces = [] if wr is None else [
            functools.partial(prepare, wr, slice(q * quarter, (q + 1) * quarter)) for q in range(4)]
        if rd is None:
            for piece in pieces:
                piece()
            return
        h = x_ref[...] + jnp.dot(rd[...], wo_ref[...], preferred_element_type=F32)
        for piece in pieces[:1]:
            piece()
        o_ref[...] = _swiglu(h, gn_ref, wg_ref, wu_ref, wd_ref, fin[0] if fin else None,
                             between=pieces[1:])

    _two_stage_pipeline(s, nblk, buf_a, buf_b, step)


def _ffn_post(h, gnorm, w_gate, w_up, w_down, post, *, tm, final_norm=None):
    bsz, seq, d = h.shape
    n = bsz * seq
    nblk = n // tm
    row = lambda v: v.reshape(1, -1)
    y, g, bonus, ln_w, ln_b, w_o = post
    params = [row(ln_w), row(ln_b), w_o, row(gnorm), w_gate, w_up, w_down]
    if final_norm is not None:
        params.append(row(final_norm))
    params, specs = _resident(params)
    cur = pl.BlockSpec((tm, d), lambda i: (jnp.minimum(i, nblk - 1), 0))
    lag = pl.BlockSpec((tm, d), lambda i: (jnp.maximum(i - 1, 0), 0))
    out = pl.pallas_call(
        functools.partial(_ffn_post_kernel, tm=tm, nblk=nblk), grid=(nblk + 1,),
        in_specs=[lag, cur, cur, cur] + specs,
        out_specs=lag, out_shape=jax.ShapeDtypeStruct((n, d), F32),
        scratch_shapes=[pltpu.VMEM((tm, d), BF16), pltpu.VMEM((tm, d), BF16)],
        compiler_params=pltpu.CompilerParams(
            dimension_semantics=("arbitrary",), vmem_limit_bytes=VMEM_LIMIT),
        name="ffn_post",
    )(h.reshape(n, d), y.reshape(n, d), g.reshape(n, d), bonus.reshape(n, d), *params)
    return out.reshape(bsz, seq, d)


def _sigmoid(x):
    return 0.5 * jnp.tanh(0.5 * x) + 0.5


def _rwkv_pre_kernel(*refs, ts, nblk, vres):
    it = iter(refs)
    (x_ref, gmix_ref, mu_ref, wr_ref, wk_ref, wv_ref, w0_ref, w1_ref, w2_ref, a0_ref, a1_ref,
     a2_ref, g1_ref, g2_ref, kk_ref, ka_ref, rk_ref) = (next(it) for _ in range(17))
    if vres:
        vf_ref, v0_ref, v1_ref, v2_ref = (next(it) for _ in range(4))
    (r_ref, k_ref, nkk_ref, kka_ref, vb_ref, lw_ref, cum_ref, g_ref,
     bonus_ref) = (next(it) for _ in range(9))
    if not vres:
        vf_out_ref = next(it)
    prev_ref, buf_a, buf_b = next(it), next(it), next(it)

    s = pl.program_id(1)

    @pl.when(s == 0)
    def _():
        prev_ref[...] = jnp.zeros(prev_ref.shape, F32)

    dot = lambda a, b_ref: jnp.dot(a, b_ref[...], preferred_element_type=F32)

    def step(rd, wr):
        has_proj, has_tail = wr is not None, rd is not None
        if has_proj:
            hn = _rms(x_ref[...], gmix_ref[...])
            first = lax.broadcasted_iota(jnp.int32, (ts, 1), 0) == 0
            hprev = jnp.where(first, prev_ref[0:1, :], pltpu.roll(hn, 1, axis=0))
            prev_ref[0:1, :] = hn[ts - 1:ts, :]
            hn_b = hn.astype(BF16)
            xx_b = (hprev - hn).astype(BF16)
            mu_b = mu_ref[...].astype(BF16)
            xr, xw, xk, xv, xa, xg = (hn_b + xx_b * mu_b[i:i + 1, :] for i in range(6))
            hw, ha, hg = dot(xw, w1_ref), dot(xa, a1_ref), dot(xg, g1_ref)
            if vres:
                hv = dot(xv, v1_ref)

        if has_tail:
            lw = -math.exp(-0.5) * _sigmoid(w0_ref[...] + rd[3])
            lw_ref[...] = lw
            hi = lw.astype(BF16)
            lo = (lw - hi.astype(F32)).astype(BF16)
            ri = lax.broadcasted_iota(jnp.int32, (ts, ts), 0)
            ci = lax.broadcasted_iota(jnp.int32, (ts, ts), 1)
            tri = jnp.where((ri // WKV_CHUNK == ci // WKV_CHUNK) & (ri >= ci),
                            1.0, 0.0).astype(BF16)

        def tail(q):
            if not has_tail:
                return
            rs = slice(q * (ts // 4), (q + 1) * (ts // 4))
            r, k, v = rd[0, rs, :], rd[1, rs, :], rd[2, rs, :]
            a = _sigmoid(a0_ref[...] + rd[4, rs, :])
            g_ref[rs, :] = rd[5, rs, :]
            if vres:
                v = v + (vf_ref[rs, :] - v) * _sigmoid(v0_ref[...] + rd[6, rs, :])
            else:
                vf_out_ref[rs, :] = v
            vb_ref[rs, :] = v.astype(BF16)
            kk = k * kk_ref[...]
            kk = kk * lax.rsqrt(jnp.maximum(_head_sum(kk * kk), 1e-24))
            ka = ka_ref[...]
            k = k * ((1.0 - ka) + a * ka)
            bonus_ref[rs, :] = _head_sum(r * k * rk_ref[...]) * v
            r_ref[rs, :] = r.astype(BF16)
            k_ref[rs, :] = k.astype(BF16)
            nkk_ref[rs, :] = (-kk).astype(BF16)
            kka_ref[rs, :] = (kk * a).astype(BF16)

        if has_proj:
            wr[0] = dot(xr, wr_ref)
        if has_tail:
            c2 = jnp.dot(tri, jnp.concatenate([hi, lo], axis=1), preferred_element_type=F32)
            d = lw.shape[1]
            cum_ref[...] = c2[:, :d] + c2[:, d:]
        tail(0)
        if has_proj:
            wr[1] = dot(xk, wk_ref)
        tail(1)
        if has_proj:
            wr[2] = dot(xv, wv_ref)
        tail(2)
        if has_proj:
            wr[3] = dot(jnp.tanh(hw).astype(BF16), w2_ref)
            wr[4] = dot(ha.astype(BF16), a2_ref)
            wr[5] = dot(_sigmoid(hg).astype(BF16), g2_ref)
            if vres:
                wr[6] = dot(hv.astype(BF16), v2_ref)
        tail(3)

    _two_stage_pipeline(s, nblk, buf_a, buf_b, step)


def _rwkv_pre(h, gmix, p, v_first, vres, *, ts):
    bsz, seq, d = h.shape
    nblk = seq // ts
    row = lambda v: v.reshape(1, -1)
    cur = pl.BlockSpec((None, ts, d), lambda b, s: (b, jnp.minimum(s, nblk - 1), 0))
    lag = pl.BlockSpec((None, ts, d), lambda b, s: (b, jnp.maximum(s - 1, 0), 0))
    params, specs = _resident(
        [row(gmix), p["mu"], p["w_r"], p["w_k"], p["w_v"], row(p["w0"]), p["w1"], p["w2"],
         row(p["a0"]), p["a1"], p["a2"], p["g1"], p["g2"], row(p["k_k"]), row(p["k_a"]),
         row(p["r_k"])])
    args, specs = [h] + params, [cur] + specs
    if vres is not None:
        v0, v1, v2 = vres
        extra, extra_specs = _resident([row(v0), v1, v2])
        args += [v_first] + extra
        specs += [lag] + extra_specs
    tok_bf = jax.ShapeDtypeStruct(h.shape, BF16)
    tok_f32 = jax.ShapeDtypeStruct(h.shape, F32)
    out_shape = [tok_bf] * 5 + [tok_f32] * (4 if vres is not None else 5)
    nproj = 7 if vres is not None else 6
    outs = pl.pallas_call(
        functools.partial(_rwkv_pre_kernel, ts=ts, nblk=nblk, vres=vres is not None),
        grid=(bsz, nblk + 1),
        in_specs=specs,
        out_specs=[lag] * len(out_shape),
        out_shape=out_shape,
        scratch_shapes=[pltpu.VMEM((8, d), F32), pltpu.VMEM((nproj, ts, d), F32),
                        pltpu.VMEM((nproj, ts, d), F32)],
        compiler_params=pltpu.CompilerParams(
            dimension_semantics=("arbitrary", "arbitrary"), vmem_limit_bytes=VMEM_LIMIT),
        name="rwkv_pre_vres" if vres is not None else "rwkv_pre",
    )(*args)
    scan_in, gate, bonus = outs[:7], outs[7], outs[8]
    if vres is None:
        v_first = outs[9]
    return scan_in, gate, bonus, v_first


def _wkv_kernel(r_ref, k_ref, nkk_ref, kka_ref, v_ref, lw_ref, cum_ref, y_ref, s_ref,
                *, nbatch, npair, nchunk):
    ch = WKV_CHUNK
    pair = 2 * ch

    @pl.when(pl.program_id(1) == 0)
    def _():
        s_ref[...] = jnp.zeros(s_ref.shape, F32)

    row = lax.broadcasted_iota(jnp.int32, (pair, pair), 0)
    col = lax.broadcasted_iota(jnp.int32, (pair, pair), 1)
    same = (row // ch) == (col // ch)
    same_bf = jnp.where(same, 1.0, 0.0).astype(BF16)
    trow = lax.broadcasted_iota(jnp.int32, (ch, pair), 0)
    tcol = lax.broadcasted_iota(jnp.int32, (ch, pair), 1) % ch
    strict = trow > tcol
    incl = trow >= tcol
    eye = jnp.where(trow == tcol, 1.0, 0.0).astype(F32)
    tn = (((0,), (0,)), ((), ()))
    dot = functools.partial(jnp.dot, preferred_element_type=F32)

    def stack(x):
        return jnp.concatenate([x, x], axis=0) * same_bf

    nchain = nbatch * npair
    items = [(ck, q) for ck in range(nchunk) for q in range(nchain)]
    sl = {(ck, q): (q // npair, slice(ck * ch, (ck + 1) * ch),
                    slice((q % npair) * LANES, (q % npair + 1) * LANES)) for ck, q in items}
    ar_lhs, v_bd, a_k, a_rb, power, tinv, bk_end, g_end = {}, {}, {}, {}, {}, {}, {}, {}
    av = {}
    for ck in range(nchunk):
        group = [(ck, q) for q in range(nchain)]
        for it in group:
            bi, rs, ls = sl[it]
            cum = cum_ref[bi, rs, ls]
            g_row = jnp.exp(cum[ch - 1:ch, :])
            g_end[it] = jnp.transpose(jnp.broadcast_to(g_row, (8, LANES)))[:, :1]
            inv = jnp.exp(-cum)
            inv_b = inv.astype(BF16)
            end_b = (inv * g_row).astype(BF16)
            k, kka = k_ref[bi, rs, ls], kka_ref[bi, rs, ls]
            bk_end[it] = jnp.concatenate([kka * end_b, k * end_b], axis=0)
            v_bd[it] = stack(v_ref[bi, rs, ls])
            ar_lhs[it] = jnp.concatenate(
                [nkk_ref[bi, rs, ls] * jnp.exp(cum - lw_ref[bi, rs, ls]).astype(BF16),
                 r_ref[bi, rs, ls] * jnp.exp(cum).astype(BF16)], axis=0)
            rhs = jnp.concatenate([stack(kka * inv_b), stack(k * inv_b)], axis=0)
            amat = dot(ar_lhs[it], rhs.T)
            power[it] = jnp.where(strict, amat[:ch, :pair], 0.0)
            tinv[it] = eye + power[it]
            a_rb[it] = jnp.where(incl, amat[ch:, :pair], 0.0).astype(BF16)
            a_k[it] = jnp.concatenate([jnp.where(strict, amat[:ch, pair:], 0.0),
                                       jnp.where(incl, amat[ch:, pair:], 0.0)],
                                      axis=0).astype(BF16)

        for it in group:
            pb = power[it].astype(BF16)
            power[it] = dot(pb, stack(pb))
        for _ in range(int(math.log2(ch)) - 2):
            for it in group:
                pb = power[it].astype(BF16)
                both = dot(jnp.concatenate([pb, tinv[it].astype(BF16)], axis=0), stack(pb))
                power[it] = both[:ch]
                tinv[it] = tinv[it] + both[ch:]
        for it in group:
            tinv[it] = (tinv[it] + dot(tinv[it].astype(BF16),
                                       stack(power[it].astype(BF16)))).astype(BF16)
        for it in group:
            av[it] = dot(a_k[it], v_bd[it])

    state = [s_ref[q] for q in range(nchain)]
    for ck in range(nchunk):
        ar, u = {}, {}
        for q in range(nchain):
            ar[q] = dot(ar_lhs[ck, q], state[q].astype(BF16)) + av[ck, q]
        for q in range(nchain):
            u[q] = dot(tinv[ck, q], stack(ar[q][:ch].astype(BF16))).astype(BF16)
        for q in range(nchain):
            bi, rs, ls = sl[ck, q]
            y_ref[bi, rs, ls] = ar[q][ch:] + dot(a_rb[ck, q], stack(u[q]))
            upd = lax.dot_general(bk_end[ck, q],
                                  jnp.concatenate([u[q], v_ref[bi, rs, ls]], axis=0),
                                  tn, preferred_element_type=F32)
            state[q] = state[q] * g_end[ck, q] + jnp.where(same, upd, 0.0)
    for q in range(nchain):
        s_ref[q] = state[q]


def _wkv(scan_in, *, npair, nchunk):
    bsz, seq, d = scan_in[0].shape
    width = npair * LANES
    rows = nchunk * WKV_CHUNK
    tok = pl.BlockSpec((bsz, rows, width), lambda p, c: (0, c, p))
    return pl.pallas_call(
        functools.partial(_wkv_kernel, nbatch=bsz, npair=npair, nchunk=nchunk),
        grid=(d // width, seq // rows),
        in_specs=[tok] * len(scan_in),
        out_specs=tok,
        out_shape=jax.ShapeDtypeStruct((bsz, seq, d), F32),
        scratch_shapes=[pltpu.VMEM((bsz * npair, LANES, LANES), F32)],
        compiler_params=pltpu.CompilerParams(
            dimension_semantics=("arbitrary", "arbitrary"), vmem_limit_bytes=VMEM_LIMIT),
        name="wkv",
    )(*scan_in)


def kernel(x, mix_norm, ffn_norm, final_norm, ab_w_in, ab_w_out, sgu_gain, sgu_w_s, sgu_bias, pool_w, pool_scale, rwkv_mu, rwkv_w_r, rwkv_w_k, rwkv_w_v, rwkv_w_o, rwkv_w0, rwkv_w1, rwkv_w2, rwkv_a0, rwkv_a1, rwkv_a2, rwkv_g1, rwkv_g2, rwkv_k_k, rwkv_k_a, rwkv_r_k, rwkv_ln_w, rwkv_ln_b, rwkv_v0, rwkv_v1, rwkv_v2, ffn_w_gate, ffn_w_up, ffn_w_down):
    depth = mix_norm.shape[0]
    assert depth % 2 == 0, "the final norm is fused into the last (RWKV) layer's FFN"
    seq = x.shape[1]
    ts_ab = min(ROWS_AB_FFN, seq)
    ts_pre = min(ROWS_RWKV_PRE, seq)
    tm = min(ROWS_FFN_POST, seq)
    bf = lambda a: a.astype(BF16)
    ab_w_in, ab_w_out, pool_w = bf(ab_w_in), bf(ab_w_out), bf(pool_w)
    ffn_w_gate, ffn_w_up, ffn_w_down = bf(ffn_w_gate), bf(ffn_w_up), bf(ffn_w_down)
    rwkv_mats = {n: bf(a) for n, a in dict(
        w_r=rwkv_w_r, w_k=rwkv_w_k, w_v=rwkv_w_v, w1=rwkv_w1, w2=rwkv_w2, a1=rwkv_a1,
        a2=rwkv_a2, g1=rwkv_g1, g2=rwkv_g2).items()}
    rwkv_w_o, rwkv_v1, rwkv_v2 = bf(rwkv_w_o), bf(rwkv_v1), bf(rwkv_v2)

    h = x
    v_first = None
    for layer in range(depth):
        i = layer // 2
        ffn_w = [_Stacked(w, layer) for w in (ffn_w_gate, ffn_w_up, ffn_w_down)]
        if layer % 2 == 0:
            h = _ab_ffn(h, mix_norm[layer], _Stacked(ab_w_in, i), _Stacked(ab_w_out, i),
                        sgu_gain[i], _Stacked(sgu_w_s, i), sgu_bias[i], _Stacked(pool_w, i),
                        pool_scale[i], ffn_norm[layer], *ffn_w, ts=ts_ab)
        else:
            p = dict(mu=rwkv_mu[i], w0=rwkv_w0[i], a0=rwkv_a0[i], k_k=rwkv_k_k[i],
                     k_a=rwkv_k_a[i], r_k=rwkv_r_k[i],
                     **{n: _Stacked(a, i) for n, a in rwkv_mats.items()})
            vres = None if i == 0 else (rwkv_v0[i - 1], _Stacked(rwkv_v1, i - 1),
                                        _Stacked(rwkv_v2, i - 1))
            scan_in, gate, bonus, v_first = _rwkv_pre(h, mix_norm[layer], p, v_first, vres,
                                                       ts=ts_pre)
            y = _wkv(scan_in, npair=WKV_PAIRS_PER_STEP, nchunk=WKV_CHUNKS_PER_STEP)
            post = (y, gate, bonus, rwkv_ln_w[i], rwkv_ln_b[i], _Stacked(rwkv_w_o, i))
            h = _ffn_post(h, ffn_norm[layer], *ffn_w, post, tm=tm,
                          final_norm=final_norm if layer == depth - 1 else None)
    return h
```
